```python
import math
import jax, jax.numpy as jnp
from jax import lax
import numpy as np

D_MODEL = 1024
BATCH = 8
SEQ = 4096
DEPTH = 4

MLA_HEADS = 8
QK_NOPE_DIM = 64
QK_ROPE_DIM = 32
QK_HEAD_DIM = QK_NOPE_DIM + QK_ROPE_DIM
V_HEAD_DIM = 64
Q_LORA_RANK = 256
KV_LORA_RANK = 256
CONV_CHANNELS = D_MODEL // 2
SHORT_CONV_WIDTH = 3
MIX_IN_DIM = Q_LORA_RANK + KV_LORA_RANK + QK_ROPE_DIM + 3 * CONV_CHANNELS
MIX_OUT_DIM = MLA_HEADS * V_HEAD_DIM + CONV_CHANNELS
ROPE_THETA = 10000.0
Q_BLOCK = 128
SSM_WIDTH = D_MODEL
SSM_GROUP = 16
SSM_GROUPS = SSM_WIDTH // SSM_GROUP
SSM_STATE = 64
DT_MIN = 1e-3
DT_MAX = 1e-1
FFN_HIDDEN = 2816
FFN_CONV_WIDTH = 3
N_EVEN = (DEPTH + 1) // 2
N_ODD = DEPTH // 2
EPS = 1e-6

kernel_name = "hybrid_mla_shortconv_s5_convffn"


def rms_norm(x, g):
    x32 = x.astype(jnp.float32)
    y = x32 * lax.rsqrt(jnp.mean(x32 * x32, axis=-1, keepdims=True) + EPS)
    return (y * g.astype(jnp.float32)).astype(x.dtype)


def causal_depthwise_conv(x, w):
    k_width, channels = w.shape
    return lax.conv_general_dilated(
        x, w[:, None, :].astype(x.dtype), window_strides=(1,), padding=[(k_width - 1, 0)],
        dimension_numbers=("NWC", "WIO", "NWC"), feature_group_count=channels)


def rope_tables(seq):
    inv_freq = 1.0 / (ROPE_THETA ** (jnp.arange(0, QK_ROPE_DIM, 2, dtype=jnp.float32) / QK_ROPE_DIM))
    ang = jnp.arange(seq, dtype=jnp.float32)[:, None] * inv_freq[None, :]
    return jnp.cos(ang)[:, None, :], jnp.sin(ang)[:, None, :]


def apply_rope(x, cos, sin):
    x1, x2 = jnp.split(x.astype(jnp.float32), 2, axis=-1)
    return jnp.concatenate([x1 * cos - x2 * sin, x2 * cos + x1 * sin], axis=-1).astype(x.dtype)


def causal_block_attention(q, k, v):
    seq = q.shape[1]
    scale = q.shape[-1] ** -0.5
    outs = []
    for i in range(seq // Q_BLOCK):
        lo, hi = i * Q_BLOCK, (i + 1) * Q_BLOCK
        s = jnp.einsum("bqhd,bkhd->bhqk", q[:, lo:hi], k[:, :hi]).astype(jnp.float32) * scale
        causal = jnp.arange(hi)[None, :] <= jnp.arange(lo, hi)[:, None]
        s = jnp.where(causal, s, -jnp.inf)
        p = jax.nn.softmax(s, axis=-1).astype(v.dtype)
        outs.append(jnp.einsum("bhqk,bkhd->bqhd", p, v[:, :hi]))
    return jnp.concatenate(outs, axis=1)


def mla_shortconv_mixer(h, w_in, cq_norm, ckv_norm, w_uq, w_ukv, q_gain, k_gain, sconv_w, w_out, cos, sin):
    bsz, seq, _ = h.shape
    proj = h @ w_in
    splits = np.cumsum([Q_LORA_RANK, KV_LORA_RANK, QK_ROPE_DIM, CONV_CHANNELS, CONV_CHANNELS]).tolist()
    c_q, c_kv, k_rope, gate_b, gate_c, conv_in = jnp.split(proj, splits, axis=-1)
    q = (rms_norm(c_q, cq_norm) @ w_uq).reshape(bsz, seq, MLA_HEADS, QK_HEAD_DIM)
    kv = (rms_norm(c_kv, ckv_norm) @ w_ukv).reshape(bsz, seq, MLA_HEADS, QK_NOPE_DIM + V_HEAD_DIM)
    k_nope, v = kv[..., :QK_NOPE_DIM], kv[..., QK_NOPE_DIM:]
    k = jnp.concatenate(
        [k_nope, jnp.broadcast_to(k_rope[:, :, None, :], (bsz, seq, MLA_HEADS, QK_ROPE_DIM))], axis=-1)
    q = rms_norm(q, q_gain)
    k = rms_norm(k, k_gain)
    q = jnp.concatenate([q[..., :QK_NOPE_DIM], apply_rope(q[..., QK_NOPE_DIM:], cos, sin)], axis=-1)
    k = jnp.concatenate([k[..., :QK_NOPE_DIM], apply_rope(k[..., QK_NOPE_DIM:], cos, sin)], axis=-1)
    attn = causal_block_attention(q, k, v).reshape(bsz, seq, MLA_HEADS * V_HEAD_DIM)
    conv = gate_b * causal_depthwise_conv(gate_c * conv_in, sconv_w)
    return jnp.concatenate([attn, conv], axis=-1) @ w_out


def _ssm_combine(earlier, later):
    ar1, ai1, br1, bi1 = earlier
    ar2, ai2, br2, bi2 = later
    return (ar2 * ar1 - ai2 * ai1,
            ar2 * ai1 + ai2 * ar1,
            ar2 * br1 - ai2 * bi1 + br2,
            ar2 * bi1 + ai2 * br1 + bi2)


def s5_mixer(h, w_in, lambda_re, lambda_im, log_step, b_re, b_im, c_re, c_im, d_skip, w_glu):
    bsz, seq, _ = h.shape
    f32 = jnp.float32
    u = (h @ w_in).astype(f32)
    ug = u.reshape(bsz, seq, SSM_GROUPS, SSM_GROUP)
    lr, li = lambda_re.astype(f32), lambda_im.astype(f32)
    dt = jnp.exp(log_step.astype(f32))[:, None]
    mag = jnp.exp(lr * dt)
    ar, ai = mag * jnp.cos(li * dt), mag * jnp.sin(li * dt)
    nr, ni = ar - 1.0, ai
    den = lr * lr + li * li
    zr, zi = (nr * lr + ni * li) / den, (ni * lr - nr * li) / den
    br_, bi_ = b_re.astype(f32), b_im.astype(f32)
    bbar_r = zr[..., None] * br_ - zi[..., None] * bi_
    bbar_i = zr[..., None] * bi_ + zi[..., None] * br_
    bu_r = jnp.einsum("gpc,bsgc->bsgp", bbar_r, ug)
    bu_i = jnp.einsum("gpc,bsgc->bsgp", bbar_i, ug)
    a_r = jnp.broadcast_to(ar, (1, seq, SSM_GROUPS, SSM_STATE))
    a_i = jnp.broadcast_to(ai, (1, seq, SSM_GROUPS, SSM_STATE))
    _, _, st_r, st_i = lax.associative_scan(_ssm_combine, (a_r, a_i, bu_r, bu_i), axis=1)
    y = (jnp.einsum("gcp,bsgp->bsgc", c_re.astype(f32), st_r)
         - jnp.einsum("gcp,bsgp->bsgc", c_im.astype(f32), st_i)).reshape(bsz, seq, SSM_WIDTH)
    y = y + d_skip.astype(f32) * u
    g = jax.nn.gelu(y).astype(h.dtype)
    a, b = jnp.split(g @ w_glu, 2, axis=-1)
    return a * jax.nn.sigmoid(b)


def conv_ffn(h, w_up, conv_w, w_down):
    up = causal_depthwise_conv(h @ w_up, conv_w)
    gate, val = jnp.split(up, 2, axis=-1)
    return (jax.nn.silu(gate) * val) @ w_down


def setup_inputs(seed: int = 0) -> dict:
    key = jax.random.key(seed)
    ks = jax.random.split(key, 32)
    f32 = jnp.float32

    def nrm(k, shape, scale):
        return jax.random.normal(k, shape, f32) * scale

    def gain(k, shape):
        return 1.0 + 0.02 * jax.random.normal(k, shape, f32)

    lam_im_base = jnp.pi * jnp.arange(SSM_STATE, dtype=f32)
    return {
        "x": nrm(ks[0], (BATCH, SEQ, D_MODEL), 1.0),
        "attn_norm": gain(ks[1], (N_EVEN, D_MODEL)),
        "mix_w_in": nrm(ks[2], (N_EVEN, D_MODEL, MIX_IN_DIM), D_MODEL ** -0.5),
        "cq_norm": gain(ks[3], (N_EVEN, Q_LORA_RANK)),
        "ckv_norm": gain(ks[4], (N_EVEN, KV_LORA_RANK)),
        "w_uq": nrm(ks[5], (N_EVEN, Q_LORA_RANK, MLA_HEADS * QK_HEAD_DIM), Q_LORA_RANK ** -0.5),
        "w_ukv": nrm(ks[6], (N_EVEN, KV_LORA_RANK, MLA_HEADS * (QK_NOPE_DIM + V_HEAD_DIM)), KV_LORA_RANK ** -0.5),
        "q_gain": gain(ks[7], (N_EVEN, QK_HEAD_DIM)),
        "k_gain": gain(ks[8], (N_EVEN, QK_HEAD_DIM)),
        "sconv_w": nrm(ks[9], (N_EVEN, SHORT_CONV_WIDTH, CONV_CHANNELS), SHORT_CONV_WIDTH ** -0.5),
        "mix_w_out": nrm(ks[10], (N_EVEN, MIX_OUT_DIM, D_MODEL), MIX_OUT_DIM ** -0.5),
        "ssm_norm": gain(ks[11], (N_ODD, D_MODEL)),
        "ssm_w_in": nrm(ks[12], (N_ODD, D_MODEL, SSM_WIDTH), D_MODEL ** -0.5),
        "lambda_re": -0.5 + 0.01 * jax.random.normal(ks[13], (N_ODD, SSM_GROUPS, SSM_STATE), f32),
        "lambda_im": lam_im_base + 0.01 * jax.random.normal(ks[14], (N_ODD, SSM_GROUPS, SSM_STATE), f32),
        "log_step": jax.random.uniform(ks[15], (N_ODD, SSM_GROUPS), f32,
                                       minval=math.log(DT_MIN), maxval=math.log(DT_MAX)),
        "b_re": nrm(ks[16], (N_ODD, SSM_GROUPS, SSM_STATE, SSM_GROUP), (2 * SSM_GROUP) ** -0.5),
        "b_im": nrm(ks[17], (N_ODD, SSM_GROUPS, SSM_STATE, SSM_GROUP), (2 * SSM_GROUP) ** -0.5),
        "c_re": nrm(ks[18], (N_ODD, SSM_GROUPS, SSM_GROUP, SSM_STATE), (2 * SSM_STATE) ** -0.5),
        "c_im": nrm(ks[19], (N_ODD, SSM_GROUPS, SSM_GROUP, SSM_STATE), (2 * SSM_STATE) ** -0.5),
        "d_skip": nrm(ks[20], (N_ODD, SSM_WIDTH), 1.0),
        "w_glu": nrm(ks[21], (N_ODD, SSM_WIDTH, 2 * D_MODEL), SSM_WIDTH ** -0.5),
        "ffn_norm": gain(ks[22], (DEPTH, D_MODEL)),
        "ffn_w_up": nrm(ks[23], (DEPTH, D_MODEL, 2 * FFN_HIDDEN), D_MODEL ** -0.5),
        "ffn_conv_w": nrm(ks[24], (DEPTH, FFN_CONV_WIDTH, 2 * FFN_HIDDEN), FFN_CONV_WIDTH ** -0.5),
        "ffn_w_down": nrm(ks[25], (DEPTH, FFN_HIDDEN, D_MODEL), FFN_HIDDEN ** -0.5),
    }


def reference(x, attn_norm, mix_w_in, cq_norm, ckv_norm, w_uq, w_ukv, q_gain, k_gain, sconv_w, mix_w_out,
              ssm_norm, ssm_w_in, lambda_re, lambda_im, log_step, b_re, b_im, c_re, c_im, d_skip, w_glu,
              ffn_norm, ffn_w_up, ffn_conv_w, ffn_w_down):
    cos, sin = rope_tables(x.shape[1])
    for layer in range(DEPTH):
        i = layer // 2
        if layer % 2 == 0:
            x = x + mla_shortconv_mixer(rms_norm(x, attn_norm[i]), mix_w_in[i], cq_norm[i], ckv_norm[i],
                                        w_uq[i], w_ukv[i], q_gain[i], k_gain[i], sconv_w[i], mix_w_out[i],
                                        cos, sin)
        else:
            x = x + s5_mixer(rms_norm(x, ssm_norm[i]), ssm_w_in[i], lambda_re[i], lambda_im[i], log_step[i],
                             b_re[i], b_im[i], c_re[i], c_im[i], d_skip[i], w_glu[i]).astype(x.dtype)
        x = x + conv_ffn(rms_norm(x, ffn_norm[layer]), ffn_w_up[layer], ffn_conv_w[layer], ffn_w_down[layer])
    return x
```

```python
import functools
import math

import jax
import jax.numpy as jnp
from jax import lax
from jax.experimental import pallas as pl
from jax.experimental.pallas import tpu as pltpu

F32 = jnp.float32
BF16 = jnp.bfloat16

EPS = 1e-6
ROPE_THETA = 10000.0
LANES = 128
MLA_HEADS = 8
QK_NOPE = 64
QK_ROPE = 32
QK_DIM = QK_NOPE + QK_ROPE
V_DIM = 64
LORA = 256
CONV_CH = 512
SSM_GROUP = 16
SSM_STATE = 64
GROUPS_PER_BLOCK = LANES // SSM_GROUP
VMEM_LIMIT = 56 * 1024 * 1024


def _params(*sem):
    return pltpu.CompilerParams(dimension_semantics=sem, vmem_limit_bytes=VMEM_LIMIT)


def _resident(shape):
    nd = len(shape)
    return pl.BlockSpec(shape, lambda *_: (0,) * nd, pipeline_mode=pl.Buffered(1))


def _rms(x, gain):
    return x * lax.rsqrt(jnp.mean(x * x, axis=-1, keepdims=True) + EPS) * gain


def _shift_conv(prev, cur, w):
    rows = cur.shape[0]
    step = prev.shape[0] // 2
    ext = jnp.concatenate([prev, cur], axis=0)
    return w[0:1] * ext[0:rows] + w[1:2] * ext[step:rows + step] + w[2:3] * cur


def _ffn_kernel(x_ref, g_ref, wup_ref, cw_ref, wdn_ref, o_ref, carry_ref, acc_ref, *, nchunk, halo):
    @pl.when(pl.program_id(0) == 0)
    def _():
        carry_ref[...] = jnp.zeros_like(carry_ref)

    x = x_ref[...]
    rows = x.shape[0]
    h = _rms(x, g_ref[...]).astype(BF16)
    acc_ref[...] = x

    def chunk(c, _):
        def half(j):
            u = jnp.dot(h, wup_ref[j, c], preferred_element_type=F32)
            prev = carry_ref[j, c]
            carry_ref[j, c] = u[rows - halo:, :]
            return _shift_conv(prev, u, cw_ref[j, c])

        gate = half(0)
        val = half(1)
        act = (gate * jax.nn.sigmoid(gate) * val).astype(BF16)
        acc_ref[...] += jnp.dot(act, wdn_ref[c], preferred_element_type=F32)
        return 0

    lax.fori_loop(0, nchunk, chunk, 0)
    o_ref[...] = acc_ref[...]


def _ffn(x, gain, w_up, conv_w, w_down, *, batch, tile_rows, fc):
    rows, d = x.shape
    hidden = w_down.shape[0]
    nchunk = hidden // fc
    halo = 2 * batch
    wup = w_up.reshape(d, 2, nchunk, fc).transpose(1, 2, 0, 3).astype(BF16)
    cw = conv_w.reshape(3, 2, nchunk, fc).transpose(1, 2, 0, 3)
    cw = jnp.pad(cw, ((0, 0), (0, 0), (0, 5), (0, 0)))
    wdn = w_down.reshape(nchunk, fc, d).astype(BF16)
    tr = min(tile_rows, rows)
    return pl.pallas_call(
        functools.partial(_ffn_kernel, nchunk=nchunk, halo=halo),
        grid=(rows // tr,),
        in_specs=[
            pl.BlockSpec((tr, d), lambda i: (i, 0)),
            _resident((1, d)),
            _resident(wup.shape),
            _resident(cw.shape),
            _resident(wdn.shape),
        ],
        out_specs=pl.BlockSpec((tr, d), lambda i: (i, 0)),
        out_shape=jax.ShapeDtypeStruct((rows, d), F32),
        scratch_shapes=[
            pltpu.VMEM((2, nchunk, halo, fc), F32),
            pltpu.VMEM((tr, d), F32),
        ],
        compiler_params=_params("arbitrary"),
        name="conv_ffn",
    )(x, gain.reshape(1, d), wup, cw, wdn)


def _rope(x, ct, sa, sb):
    return x * ct + pltpu.roll(x, LANES - QK_ROPE // 2, 1) * sa + pltpu.roll(x, QK_ROPE // 2, 1) * sb


def _mixproj_kernel(x_ref, g_ref, win_ref, cqg_ref, ckvg_ref, wuq_ref, wukv_ref, qg_ref, kg_ref,
                    scw_ref, ct_ref, sa_ref, sb_ref,
                    q_ref, k_ref, v_ref, cv_ref, carry_ref, *, halo):
    @pl.when(pl.program_id(0) == 0)
    def _():
        carry_ref[...] = jnp.zeros_like(carry_ref)

    x = x_ref[...]
    rows = x.shape[0]
    h = _rms(x, g_ref[...]).astype(BF16)
    proj = jnp.dot(h, win_ref[...], preferred_element_type=F32)
    o = 0
    c_q = proj[:, o:o + LORA]; o += LORA
    c_kv = proj[:, o:o + LORA]; o += LORA
    k_rope = proj[:, o:o + LANES]; o += LANES
    gate_b = proj[:, o:o + CONV_CH]; o += CONV_CH
    gate_c = proj[:, o:o + CONV_CH]; o += CONV_CH
    conv_in = proj[:, o:o + CONV_CH]

    m = gate_c * conv_in
    prev = carry_ref[...]
    carry_ref[...] = m[rows - halo:, :]
    cv_ref[...] = (gate_b * _shift_conv(prev, m, scw_ref[...])).astype(BF16)

    q = jnp.dot(_rms(c_q, cqg_ref[...]).astype(BF16), wuq_ref[...], preferred_element_type=F32)
    kv = jnp.dot(_rms(c_kv, ckvg_ref[...]).astype(BF16), wukv_ref[...], preferred_element_type=F32)
    v_ref[...] = kv[:, MLA_HEADS * LANES:].astype(BF16)
    ct, sa, sb = ct_ref[...], sa_ref[...], sb_ref[...]
    qg, kg = qg_ref[...], kg_ref[...]
    for hd in range(MLA_HEADS):
        sl = slice(hd * LANES, (hd + 1) * LANES)
        qh = q[:, sl]
        qn = qh * lax.rsqrt(jnp.sum(qh * qh, axis=-1, keepdims=True) * (1.0 / QK_DIM) + EPS) * qg
        q_ref[:, sl] = _rope(qn, ct, sa, sb).astype(BF16)
        kh = kv[:, sl] + k_rope
        kn = kh * lax.rsqrt(jnp.sum(kh * kh, axis=-1, keepdims=True) * (1.0 / QK_DIM) + EPS) * kg
        k_ref[:, sl] = _rope(kn, ct, sa, sb).astype(BF16)


def _mixproj(x, gain, w_in, cq_norm, ckv_norm, w_uq, w_ukv, q_gain, k_gain, sconv_w, tables,
             *, batch, tile_rows):
    rows, d = x.shape
    halo = 2 * batch
    zeros = lambda n: jnp.zeros((d, n), F32)
    o1, o2 = 2 * LORA, 2 * LORA + QK_ROPE
    win = jnp.concatenate(
        [w_in[:, :o1], zeros(QK_NOPE), w_in[:, o1:o2], zeros(LANES - QK_DIM), w_in[:, o2:]],
        axis=1).astype(BF16)
    wuq = jnp.pad(w_uq.reshape(LORA, MLA_HEADS, QK_DIM), ((0, 0), (0, 0), (0, LANES - QK_DIM)))
    wuq = wuq.reshape(LORA, MLA_HEADS * LANES).astype(BF16)
    wukv = w_ukv.reshape(LORA, MLA_HEADS, QK_NOPE + V_DIM)
    wk = jnp.pad(wukv[:, :, :QK_NOPE], ((0, 0), (0, 0), (0, LANES - QK_NOPE)))
    wukv = jnp.concatenate([wk.reshape(LORA, MLA_HEADS * LANES),
                            wukv[:, :, QK_NOPE:].reshape(LORA, MLA_HEADS * V_DIM)], axis=1).astype(BF16)
    qg = jnp.pad(q_gain * (QK_DIM ** -0.5), (0, LANES - QK_DIM)).reshape(1, LANES)
    kg = jnp.pad(k_gain, (0, LANES - QK_DIM)).reshape(1, LANES)
    scw = jnp.pad(sconv_w, ((0, 5), (0, 0)))
    tr = min(tile_rows, rows)
    row_spec = lambda n: pl.BlockSpec((tr, n), lambda i: (i, 0))
    hl = MLA_HEADS * LANES
    return pl.pallas_call(
        functools.partial(_mixproj_kernel, halo=halo),
        grid=(rows // tr,),
        in_specs=[
            row_spec(d),
            _resident((1, d)),
            _resident(win.shape),
            _resident((1, LORA)),
            _resident((1, LORA)),
            _resident(wuq.shape),
            _resident(wukv.shape),
            _resident((1, LANES)),
            _resident((1, LANES)),
            _resident(scw.shape),
            row_spec(LANES), row_spec(LANES), row_spec(LANES),
        ],
        out_specs=[row_spec(hl), row_spec(hl), row_spec(MLA_HEADS * V_DIM), row_spec(CONV_CH)],
        out_shape=[
            jax.ShapeDtypeStruct((rows, hl), BF16),
            jax.ShapeDtypeStruct((rows, hl), BF16),
            jax.ShapeDtypeStruct((rows, MLA_HEADS * V_DIM), BF16),
            jax.ShapeDtypeStruct((rows, CONV_CH), BF16),
        ],
        scratch_shapes=[pltpu.VMEM((halo, CONV_CH), F32)],
        compiler_params=_params("arbitrary"),
        name="mix_proj",
    )(x, gain.reshape(1, d), win, cq_norm.reshape(1, LORA), ckv_norm.reshape(1, LORA), wuq, wukv,
      qg, kg, scw, *tables)


def _rope_tables(seq, batch):
    half = QK_ROPE // 2
    inv_freq = 1.0 / (ROPE_THETA ** (jnp.arange(0, QK_ROPE, 2, dtype=F32) / QK_ROPE))
    ang = jnp.arange(seq, dtype=F32)[:, None] * inv_freq[None, :]
    cos, sin = jnp.cos(ang), jnp.sin(ang)
    z = lambda n: jnp.zeros((seq, n), F32)
    tail = LANES - QK_DIM
    ct = jnp.concatenate([jnp.ones((seq, QK_NOPE), F32), cos, cos, z(tail)], axis=1)
    sa = jnp.concatenate([z(QK_NOPE), -sin, z(half), z(tail)], axis=1)
    sb = jnp.concatenate([z(QK_NOPE), z(half), sin, z(tail)], axis=1)
    return tuple(jnp.repeat(t, batch, axis=0) for t in (ct, sa, sb))


def _attn_kernel(q_ref, k_ref, v_ref, o_ref, *, tq):
    qi = pl.program_id(2)
    lane = lax.broadcasted_iota(jnp.int32, (1, LANES), 1)
    row = lax.broadcasted_iota(jnp.int32, (tq, tq), 0)
    col = lax.broadcasted_iota(jnp.int32, (tq, tq), 1)
    outs = []
    for hh in range(2):
        sl = slice(hh * LANES, (hh + 1) * LANES)
        qh = q_ref[:, sl]

        def block(kb, carry, masked):
            m, l, acc = carry
            start = pl.multiple_of(kb * tq, tq)
            kblk = k_ref[pl.ds(start, tq), sl]
            s = lax.dot_general(qh, kblk, (((1,), (1,)), ((), ())), preferred_element_type=F32)
            if masked:
                s = jnp.where(col <= row, s, -jnp.inf)
            m_new = jnp.maximum(m, jnp.max(s, axis=-1, keepdims=True))
            alpha = jnp.exp(m - m_new)
            p = jnp.exp(s - m_new)
            l = alpha * l + jnp.sum(p, axis=-1, keepdims=True)
            pv = jnp.dot(p.astype(BF16), v_ref[pl.ds(start, tq), :], preferred_element_type=F32)
            return m_new, l, alpha * acc + pv

        init = (jnp.full((tq, 1), -jnp.inf, F32), jnp.zeros((tq, 1), F32), jnp.zeros((tq, LANES), F32))
        carry = lax.fori_loop(0, qi, functools.partial(block, masked=False), init)
        _, l, acc = block(qi, carry, True)
        outs.append(acc / l)
    o_ref[...] = jnp.where(lane < V_DIM, outs[0], outs[1]).astype(o_ref.dtype)


def _attention(q, k, v, *, seq, batch, tq):
    tq = min(tq, seq)
    pairs = MLA_HEADS // 2
    q2 = q.reshape(seq, batch * MLA_HEADS * LANES)
    k2 = k.reshape(seq, batch * MLA_HEADS * LANES)
    v2 = v.reshape(seq, batch * MLA_HEADS * V_DIM)
    col = lambda b, j, i: b * pairs + j
    out = pl.pallas_call(
        functools.partial(_attn_kernel, tq=tq),
        grid=(batch, pairs, seq // tq),
        in_specs=[
            pl.BlockSpec((tq, 2 * LANES), lambda b, j, i: (i, col(b, j, i))),
            pl.BlockSpec((seq, 2 * LANES), lambda b, j, i: (0, col(b, j, i))),
            pl.BlockSpec((seq, LANES), lambda b, j, i: (0, col(b, j, i))),
        ],
        out_specs=pl.BlockSpec((tq, LANES), lambda b, j, i: (i, col(b, j, i))),
        out_shape=jax.ShapeDtypeStruct((seq, batch * MLA_HEADS * V_DIM), BF16),
        compiler_params=_params("arbitrary", "arbitrary", "arbitrary"),
        name="causal_attention",
    )(q2, k2, v2)
    return out.reshape(seq * batch, MLA_HEADS * V_DIM)


def _outproj_kernel(x_ref, a_ref, c_ref, wa_ref, wc_ref, o_ref):
    o_ref[...] = (x_ref[...]
                  + jnp.dot(a_ref[...], wa_ref[...], preferred_element_type=F32)
                  + jnp.dot(c_ref[...], wc_ref[...], preferred_element_type=F32))


def _outproj(x, attn, conv, w_out, *, tile_rows):
    rows, d = x.shape
    na, nc = attn.shape[1], conv.shape[1]
    wa = w_out[:na].astype(BF16)
    wc = w_out[na:].astype(BF16)
    tr = min(tile_rows, rows)
    row_spec = lambda n: pl.BlockSpec((tr, n), lambda i: (i, 0))
    return pl.pallas_call(
        _outproj_kernel,
        grid=(rows // tr,),
        in_specs=[row_spec(d), row_spec(na), row_spec(nc), _resident(wa.shape), _resident(wc.shape)],
        out_specs=row_spec(d),
        out_shape=jax.ShapeDtypeStruct((rows, d), F32),
        compiler_params=_params("arbitrary"),
        name="mix_out_proj",
    )(x, attn, conv, wa, wc)


def _s5_discretize_kernel(lr_ref, li_ref, dt_ref, lre_ref, lie_ref, dte_ref, bre_ref, bim_ref,
                          ar_ref, ai_ref, bbr_ref, bbi_ref):
    def zoh(lr, li, dt):
        mag = jnp.exp(lr * dt)
        ar, ai = mag * jnp.cos(li * dt), mag * jnp.sin(li * dt)
        nr, ni = ar - 1.0, ai
        den = lr * lr + li * li
        return ar, ai, (nr * lr + ni * li) / den, (ni * lr - nr * li) / den

    ar, ai, _, _ = zoh(lr_ref[...], li_ref[...], dt_ref[...])
    ar_ref[...] = ar
    ai_ref[...] = ai
    _, _, zr, zi = zoh(lre_ref[...], lie_ref[...], dte_ref[...])
    br, bi = bre_ref[...], bim_ref[...]
    bbr_ref[...] = zr * br - zi * bi
    bbi_ref[...] = zr * bi + zi * br


def _s5_discretize(lambda_re, lambda_im, log_step, b_re, b_im):
    g, p = lambda_re.shape
    c = b_re.shape[-1]
    dt = jnp.broadcast_to(jnp.exp(log_step)[:, None], (g, p))
    expand = lambda a: jnp.repeat(a, c, axis=1)
    flat = lambda a: a.reshape(g, p * c)
    out = pl.pallas_call(
        _s5_discretize_kernel,
        out_shape=[jax.ShapeDtypeStruct((g, p), F32)] * 2 + [jax.ShapeDtypeStruct((g, p * c), F32)] * 2,
        name="s5_discretize",
    )(lambda_re, lambda_im, dt, expand(lambda_re), expand(lambda_im), expand(dt), flat(b_re), flat(b_im))
    ar, ai, bbr, bbi = out
    return ar, ai, bbr.reshape(g, p, c), bbi.reshape(g, p, c)


def _s5_kernel(x_ref, g_ref, win_ref, bm_ref, a_ref, cm_ref, d_ref, wglu_ref, o_ref,
               state_ref, u_ref, bu_ref, st_ref, y_ref, *, batch, nblock, half):
    @pl.when(pl.program_id(0) == 0)
    def _():
        state_ref[...] = jnp.zeros_like(state_ref)

    x = x_ref[...]
    rows, d = x.shape
    steps = rows // batch
    h = _rms(x, g_ref[...]).astype(BF16)
    u = jnp.dot(h, win_ref[...], preferred_element_type=F32)
    u_ref[...] = u
    ub = u.astype(BF16)
    for blk in range(nblock):
        sl = slice(blk * LANES, (blk + 1) * LANES)
        bu_ref[...] = jnp.dot(ub[:, sl], bm_ref[blk], preferred_element_type=F32)
        ar = a_ref[blk, 0:batch, :]
        ai = a_ref[blk, batch:2 * batch, :]

        def step(t, carry):
            sr, si = carry
            r0 = pl.multiple_of(t * batch, batch)
            br = bu_ref[pl.ds(r0, batch), 0:half]
            bi = bu_ref[pl.ds(r0, batch), half:2 * half]
            nr = ar * sr - ai * si + br
            ni = ar * si + ai * sr + bi
            st_ref[pl.ds(r0, batch), 0:half] = nr
            st_ref[pl.ds(r0, batch), half:2 * half] = ni
            return nr, ni

        sr, si = lax.fori_loop(0, steps, step,
                               (state_ref[blk, 0:batch, :], state_ref[blk, batch:2 * batch, :]),
                               unroll=4)
        state_ref[blk, 0:batch, :] = sr
        state_ref[blk, batch:2 * batch, :] = si
        y_ref[:, sl] = jnp.dot(st_ref[...].astype(BF16), cm_ref[blk], preferred_element_type=F32)
    y = y_ref[...] + d_ref[...] * u_ref[...]
    act = jax.nn.gelu(y).astype(BF16)
    z = jnp.dot(act, wglu_ref[...], preferred_element_type=F32)
    o_ref[...] = x + z[:, :d] * jax.nn.sigmoid(z[:, d:])


def _s5(x, gain, w_in, lambda_re, lambda_im, log_step, b_re, b_im, c_re, c_im, d_skip, w_glu,
        *, batch, tile_rows):
    rows, d = x.shape
    groups, nstate = lambda_re.shape
    gpb = GROUPS_PER_BLOCK
    nblock = groups // gpb
    half = gpb * nstate
    ar, ai, bbr, bbi = _s5_discretize(lambda_re, lambda_im, log_step, b_re, b_im)
    eye = jnp.eye(gpb, dtype=F32)

    def in_mat(bb):
        bb = bb.reshape(nblock, gpb, nstate, SSM_GROUP)
        return jnp.einsum("Ggpc,hg->Ghcgp", bb, eye).reshape(nblock, LANES, half)

    def out_mat(cc):
        cc = cc.reshape(nblock, gpb, SSM_GROUP, nstate)
        return jnp.einsum("Ggcp,gh->Ggphc", cc, eye).reshape(nblock, half, LANES)

    bm = jnp.concatenate([in_mat(bbr), in_mat(bbi)], axis=2).astype(BF16)
    cm = jnp.concatenate([out_mat(c_re), out_mat(-c_im)], axis=1).astype(BF16)
    rep = lambda a: jnp.broadcast_to(a.reshape(nblock, 1, half), (nblock, batch, half))
    a_mat = jnp.concatenate([rep(ar), rep(ai)], axis=1)
    tr = min(tile_rows, rows)
    return pl.pallas_call(
        functools.partial(_s5_kernel, batch=batch, nblock=nblock, half=half),
        grid=(rows // tr,),
        in_specs=[
            pl.BlockSpec((tr, d), lambda i: (i, 0)),
            _resident((1, d)),
            _resident((d, d)),
            _resident(bm.shape),
            _resident(a_mat.shape),
            _resident(cm.shape),
            _resident((1, d)),
            _resident(w_glu.shape),
        ],
        out_specs=pl.BlockSpec((tr, d), lambda i: (i, 0)),
        out_shape=jax.ShapeDtypeStruct((rows, d), F32),
        scratch_shapes=[
            pltpu.VMEM((nblock, 2 * batch, half), F32),
            pltpu.VMEM((tr, d), F32),
            pltpu.VMEM((tr, 2 * half), F32),
            pltpu.VMEM((tr, 2 * half), F32),
            pltpu.VMEM((tr, d), F32),
        ],
        compiler_params=_params("arbitrary"),
        name="s5_mixer",
    )(x, gain.reshape(1, d), w_in.astype(BF16), bm, a_mat, cm, d_skip.reshape(1, d), w_glu.astype(BF16))


def kernel(x, attn_norm, mix_w_in, cq_norm, ckv_norm, w_uq, w_ukv, q_gain, k_gain, sconv_w, mix_w_out,
           ssm_norm, ssm_w_in, lambda_re, lambda_im, log_step, b_re, b_im, c_re, c_im, d_skip, w_glu,
           ffn_norm, ffn_w_up, ffn_conv_w, ffn_w_down):
    batch, seq, d = x.shape
    depth = ffn_norm.shape[0]
    xt = x.transpose(1, 0, 2).reshape(seq * batch, d)
    tables = _rope_tables(seq, batch)
    for layer in range(depth):
        i = layer // 2
        if layer % 2 == 0:
            q, k, v, conv = _mixproj(xt, attn_norm[i], mix_w_in[i], cq_norm[i], ckv_norm[i], w_uq[i],
                                     w_ukv[i], q_gain[i], k_gain[i], sconv_w[i], tables,
                                     batch=batch, tile_rows=512)
            attn = _attention(q, k, v, seq=seq, batch=batch, tq=512)
            xt = _outproj(xt, attn, conv, mix_w_out[i], tile_rows=1024)
        else:
            xt = _s5(xt, ssm_norm[i], ssm_w_in[i], lambda_re[i], lambda_im[i], log_step[i],
                     b_re[i], b_im[i], c_re[i], c_im[i], d_skip[i], w_glu[i],
                     batch=batch, tile_rows=512)
        xt = _ffn(xt, ffn_norm[layer], ffn_w_up[layer], ffn_conv_w[layer], ffn_w_down[layer],
                  batch=batch, tile_rows=512, fc=256)
    return xt.reshape(seq, batch, d).transpose(1, 0, 2)
```

```python
import functools
import math

import jax
import jax.numpy as jnp
from jax import lax
from jax.experimental import pallas as pl
from jax.experimental.pallas import tpu as pltpu

F32 = jnp.float32
BF16 = jnp.bfloat16

EPS = 1e-6
ROPE_THETA = 10000.0
LANES = 128
MXU_TILE = 256
MLA_HEADS = 8
QK_NOPE = 64
QK_ROPE = 32
QK_DIM = QK_NOPE + QK_ROPE
V_DIM = 64
LORA = 256
CONV_CH = 512
SSM_GROUP = 16
SSM_STATE = 64
GROUPS_PER_BLOCK = LANES // SSM_GROUP
HEAD_LANES = MLA_HEADS * LANES
VMEM_LIMIT = 56 * 1024 * 1024


def _params(*sem):
    return pltpu.CompilerParams(dimension_semantics=sem, vmem_limit_bytes=VMEM_LIMIT)


def _resident(shape):
    nd = len(shape)
    return pl.BlockSpec(shape, lambda *_: (0,) * nd, pipeline_mode=pl.Buffered(1))


def _rms(x, gain):
    return x * lax.rsqrt(jnp.mean(x * x, axis=-1, keepdims=True) + EPS) * gain


def _shift_conv(prev, cur, w):
    rows = cur.shape[0]
    step = prev.shape[0] // 2
    ext = jnp.concatenate([prev, cur], axis=0)
    return w[0:1] * ext[0:rows] + w[1:2] * ext[step:rows + step] + w[2:3] * cur


def _ffn_kernel(*refs, nchunk, halo, batch, mix):
    if mix:
        (x_ref, a_ref, cv_ref, wa_ref, wc_ref, g_ref, wup_ref, cw_ref, wdn_ref,
         o_ref, carry_ref, acc_ref, ail_ref) = refs
    else:
        x_ref, g_ref, wup_ref, cw_ref, wdn_ref, o_ref, carry_ref, acc_ref = refs

    @pl.when(pl.program_id(0) == 0)
    def _():
        carry_ref[...] = jnp.zeros_like(carry_ref)

    x = x_ref[...]
    rows = x.shape[0]
    if mix:
        steps = rows // batch
        nblk = a_ref.shape[1] // (batch * LANES)
        for b in range(batch):
            for c in range(nblk):
                col = (b * nblk + c) * LANES
                ail_ref[c, pl.ds(b, steps, stride=batch), :] = a_ref[:, col:col + LANES]
        attn = jnp.concatenate([ail_ref[c] for c in range(nblk)], axis=1).astype(BF16)
        x = (x + jnp.dot(attn, wa_ref[...], preferred_element_type=F32)
             + jnp.dot(cv_ref[...], wc_ref[...], preferred_element_type=F32))
    h = _rms(x, g_ref[...]).astype(BF16)
    acc_ref[...] = x

    def chunk(c, _):
        def half(j):
            u = jnp.dot(h, wup_ref[j, c], preferred_element_type=F32)
            prev = carry_ref[j, c]
            carry_ref[j, c] = u[rows - halo:, :]
            return _shift_conv(prev, u, cw_ref[j, c])

        gate = half(0)
        val = half(1)
        act = (gate * jax.nn.sigmoid(gate) * val).astype(BF16)
        acc_ref[...] += jnp.dot(act, wdn_ref[c], preferred_element_type=F32)
        return 0

    lax.fori_loop(0, nchunk, chunk, 0)
    o_ref[...] = acc_ref[...]


def _ffn(x, gain, w_up, conv_w, w_down, *, batch, tile_rows, fc, mixer=None):
    rows, d = x.shape
    hidden = w_down.shape[0]
    nchunk = hidden // fc
    halo = 2 * batch
    wup = w_up.reshape(d, 2, nchunk, fc).transpose(1, 2, 0, 3).astype(BF16)
    cw = conv_w.reshape(3, 2, nchunk, fc).transpose(1, 2, 0, 3)
    cw = jnp.pad(cw, ((0, 0), (0, 0), (0, 5), (0, 0)))
    wdn = w_down.reshape(nchunk, fc, d).astype(BF16)
    tr = min(tile_rows, rows)
    row_spec = lambda n: pl.BlockSpec((tr, n), lambda i: (i, 0))
    operands = [x]
    in_specs = [row_spec(d)]
    scratch = [pltpu.VMEM((2, nchunk, halo, fc), F32), pltpu.VMEM((tr, d), F32)]
    if mixer is not None:
        attn, conv, w_out = mixer
        na = attn.shape[1] // batch
        wa = w_out[:na].astype(BF16)
        wc = w_out[na:].astype(BF16)
        operands += [attn, conv, wa, wc]
        in_specs += [pl.BlockSpec((tr // batch, batch * na), lambda i: (i, 0)), row_spec(conv.shape[1]),
                     _resident(wa.shape), _resident(wc.shape)]
        scratch.append(pltpu.VMEM((na // LANES, tr, LANES), F32))
    operands += [gain.reshape(1, d), wup, cw, wdn]
    in_specs += [_resident((1, d)), _resident(wup.shape), _resident(cw.shape), _resident(wdn.shape)]
    return pl.pallas_call(
        functools.partial(_ffn_kernel, nchunk=nchunk, halo=halo, batch=batch, mix=mixer is not None),
        grid=(rows // tr,),
        in_specs=in_specs,
        out_specs=row_spec(d),
        out_shape=jax.ShapeDtypeStruct((rows, d), F32),
        scratch_shapes=scratch,
        compiler_params=_params("arbitrary"),
        name="conv_ffn",
    )(*operands)


def _head_sumsq(x, ones_ref):
    sq = (x * x).astype(BF16)
    parts = [jnp.dot(sq[:, c:c + MXU_TILE], ones_ref[...], preferred_element_type=F32)
             for c in range(0, x.shape[1], MXU_TILE)]
    return jnp.concatenate(parts, axis=1)


def _mixproj_kernel(x_ref, g_ref, win_ref, cqg_ref, ckvg_ref, wuq_ref, wukv_ref, gains_ref, ones_ref,
                    scw_ref, ct_ref, st_ref,
                    q_ref, k_ref, v_ref, cv_ref, carry_ref, qs_ref, ks_ref, vs_ref, *, halo, batch):
    @pl.when(pl.program_id(0) == 0)
    def _():
        carry_ref[...] = jnp.zeros_like(carry_ref)

    x = x_ref[...]
    rows = x.shape[0]
    steps = rows // batch
    h = _rms(x, g_ref[...]).astype(BF16)
    proj = jnp.dot(h, win_ref[...], preferred_element_type=F32)
    o = 0
    c_q = proj[:, o:o + LORA]; o += LORA
    c_kv = proj[:, o:o + LORA]; o += LORA
    k_rope = proj[:, o:o + LANES]; o += LANES
    k_rope_rot = proj[:, o:o + LANES]; o += LANES
    gate_b = proj[:, o:o + CONV_CH]; o += CONV_CH
    gate_c = proj[:, o:o + CONV_CH]; o += CONV_CH
    conv_in = proj[:, o:o + CONV_CH]

    m = gate_c * conv_in
    prev = carry_ref[...]
    carry_ref[...] = m[rows - halo:, :]
    cv_ref[...] = (gate_b * _shift_conv(prev, m, scw_ref[...])).astype(BF16)

    qq = jnp.dot(_rms(c_q, cqg_ref[...]).astype(BF16), wuq_ref[...], preferred_element_type=F32)
    kv = jnp.dot(_rms(c_kv, ckvg_ref[...]).astype(BF16), wukv_ref[...], preferred_element_type=F32)
    q, q_rot = qq[:, :HEAD_LANES], qq[:, HEAD_LANES:]
    kn, v = kv[:, :HEAD_LANES], kv[:, HEAD_LANES:]
    ct, st = ct_ref[...], st_ref[...]
    gains = gains_ref[...]
    tq, sq = ct * gains[0:1], st * gains[1:2]
    tk, sk = ct * gains[2:3], st * gains[3:4]
    k_nope_gain, v_ones = gains[4:5], gains[5:6]
    inv_dim = 1.0 / QK_DIM
    q_inv = lax.rsqrt(_head_sumsq(q, ones_ref) * inv_dim + EPS)
    kr_sumsq = jnp.sum(k_rope * k_rope, axis=-1, keepdims=True)
    k_inv = lax.rsqrt((_head_sumsq(kn, ones_ref) + kr_sumsq) * inv_dim + EPS)
    kr = k_rope * tk + k_rope_rot * sk
    for hd in range(MLA_HEADS):
        sl = slice(hd * LANES, (hd + 1) * LANES)
        qs_ref[hd] = (q[:, sl] * tq + q_rot[:, sl] * sq) * q_inv[:, sl]
        ks_ref[hd] = (kn[:, sl] * k_nope_gain + kr) * k_inv[:, sl]
        vs_ref[hd] = v[:, sl] + v_ones
    for src, dst in ((qs_ref, q_ref), (ks_ref, k_ref), (vs_ref, v_ref)):
        for b in range(batch):
            for hd in range(MLA_HEADS):
                col = (b * MLA_HEADS + hd) * LANES
                dst[:, col:col + LANES] = src[hd, pl.ds(b, steps, stride=batch), :].astype(BF16)


def _mixproj(x, gain, w_in, cq_norm, ckv_norm, w_uq, w_ukv, q_gain, k_gain, sconv_w, tables,
             *, batch, tile_rows):
    rows, d = x.shape
    halo = 2 * batch
    half = QK_ROPE // 2
    tail = LANES - QK_DIM
    zeros = lambda n: jnp.zeros((d, n), F32)
    o1, o2 = 2 * LORA, 2 * LORA + QK_ROPE
    w_kr = w_in[:, o1:o2]
    win = jnp.concatenate(
        [w_in[:, :o1],
         zeros(QK_NOPE), w_kr, zeros(tail),
         zeros(QK_NOPE), -w_kr[:, half:], w_kr[:, :half], zeros(tail),
         w_in[:, o2:]], axis=1).astype(BF16)
    wq = w_uq.reshape(LORA, MLA_HEADS, QK_DIM)
    pad3 = lambda a: jnp.pad(a, ((0, 0), (0, 0), (0, LANES - a.shape[2]))).reshape(LORA, HEAD_LANES)
    wq_rot = jnp.concatenate([jnp.zeros_like(wq[:, :, :QK_NOPE]), -wq[:, :, QK_NOPE + half:],
                              wq[:, :, QK_NOPE:QK_NOPE + half]], axis=2)
    wuq = jnp.concatenate([pad3(wq), pad3(wq_rot)], axis=1).astype(BF16)
    wkv = w_ukv.reshape(LORA, MLA_HEADS, QK_NOPE + V_DIM)
    wukv = jnp.concatenate([pad3(wkv[:, :, :QK_NOPE]), pad3(wkv[:, :, QK_NOPE:])], axis=1).astype(BF16)

    def rope_gains(g):
        z = jnp.zeros((tail,), F32)
        return (jnp.concatenate([g, z]),
                jnp.concatenate([jnp.zeros((QK_NOPE,), F32), g[QK_NOPE + half:], g[QK_NOPE:QK_NOPE + half], z]))

    gq = rope_gains(q_gain * (QK_DIM ** -0.5 * math.log2(math.e)))
    gk = rope_gains(k_gain)
    k_nope_gain = jnp.concatenate([k_gain[:QK_NOPE], jnp.zeros((LANES - QK_NOPE,), F32)])
    v_ones = (jnp.arange(LANES) == V_DIM).astype(F32)
    gains = jnp.stack([*gq, *gk, k_nope_gain, v_ones, jnp.zeros((LANES,), F32), jnp.zeros((LANES,), F32)])
    lane_head = jnp.arange(MXU_TILE) // LANES
    ones_bd = (lane_head[:, None] == lane_head[None, :]).astype(BF16)
    scw = jnp.pad(sconv_w, ((0, 5), (0, 0)))
    tr = min(tile_rows, rows)
    ts = tr // batch
    row_spec = lambda n: pl.BlockSpec((tr, n), lambda i: (i, 0))
    wide_spec = pl.BlockSpec((ts, batch * HEAD_LANES), lambda i: (i, 0))
    wide_shape = jax.ShapeDtypeStruct((rows // batch, batch * HEAD_LANES), BF16)
    return pl.pallas_call(
        functools.partial(_mixproj_kernel, halo=halo, batch=batch),
        grid=(rows // tr,),
        in_specs=[
            row_spec(d),
            _resident((1, d)),
            _resident(win.shape),
            _resident((1, LORA)),
            _resident((1, LORA)),
            _resident(wuq.shape),
            _resident(wukv.shape),
            _resident(gains.shape),
            _resident(ones_bd.shape),
            _resident(scw.shape),
            row_spec(LANES), row_spec(LANES),
        ],
        out_specs=[wide_spec, wide_spec, wide_spec, row_spec(CONV_CH)],
        out_shape=[wide_shape, wide_shape, wide_shape, jax.ShapeDtypeStruct((rows, CONV_CH), BF16)],
        scratch_shapes=[pltpu.VMEM((halo, CONV_CH), F32)] + [pltpu.VMEM((MLA_HEADS, tr, LANES), F32)] * 3,
        compiler_params=_params("arbitrary"),
        name="mix_proj",
    )(x, gain.reshape(1, d), win, cq_norm.reshape(1, LORA), ckv_norm.reshape(1, LORA), wuq, wukv,
      gains, ones_bd, scw, *tables)


def _rope_tables(seq, batch):
    inv_freq = 1.0 / (ROPE_THETA ** (jnp.arange(0, QK_ROPE, 2, dtype=F32) / QK_ROPE))
    ang = jnp.arange(seq, dtype=F32)[:, None] * inv_freq[None, :]
    cos, sin = jnp.cos(ang), jnp.sin(ang)
    z = lambda n: jnp.zeros((seq, n), F32)
    tail = LANES - QK_DIM
    ct = jnp.concatenate([jnp.ones((seq, QK_NOPE), F32), cos, cos, z(tail)], axis=1)
    st = jnp.concatenate([z(QK_NOPE), sin, sin, z(tail)], axis=1)
    return tuple(jnp.repeat(t, batch, axis=0) for t in (ct, st))


def _attn_kernel(q_ref, k_ref, v_ref, o_ref, *, tq):
    qi = pl.program_id(2)
    lane = lax.broadcasted_iota(jnp.int32, (1, LANES), 1)
    row = lax.broadcasted_iota(jnp.int32, (tq, tq), 0)
    col = lax.broadcasted_iota(jnp.int32, (tq, tq), 1)
    heads = [slice(hh * LANES, (hh + 1) * LANES) for hh in range(2)]
    qs = [q_ref[:, sl] for sl in heads]

    def block(kb, carry, masked):
        start = pl.multiple_of(kb * tq, tq)
        out = []
        for hh, sl in enumerate(heads):
            m, acc = carry[hh]
            s = lax.dot_general(qs[hh], k_ref[pl.ds(start, tq), sl], (((1,), (1,)), ((), ())),
                                preferred_element_type=F32)
            if masked:
                s = jnp.where(col <= row, s, -jnp.inf)
            m_new = jnp.maximum(m, jnp.max(s, axis=-1, keepdims=True))
            p = jnp.exp2(s - m_new).astype(BF16)
            pv = jnp.dot(p, v_ref[pl.ds(start, tq), sl], preferred_element_type=F32)
            out.append((m_new, jnp.exp2(m - m_new) * acc + pv))
        return tuple(out)

    init = tuple((jnp.full((tq, 1), -jnp.inf, F32), jnp.zeros((tq, LANES), F32)) for _ in heads)
    carry = lax.fori_loop(0, qi, functools.partial(block, masked=False), init)
    (_, acc0), (_, acc1) = block(qi, carry, True)
    o0 = acc0 / acc0[:, V_DIM:V_DIM + 1]
    o1 = acc1 / acc1[:, V_DIM:V_DIM + 1]
    o_ref[...] = jnp.where(lane < V_DIM, o0, pltpu.roll(o1, V_DIM, 1))


def _attention(q, k, v, *, seq, batch, tq):
    tq = min(tq, seq)
    pairs = MLA_HEADS // 2
    col = lambda b, j, i: b * pairs + j
    return pl.pallas_call(
        functools.partial(_attn_kernel, tq=tq),
        grid=(batch, pairs, seq // tq),
        in_specs=[
            pl.BlockSpec((tq, 2 * LANES), lambda b, j, i: (i, col(b, j, i))),
            pl.BlockSpec((seq, 2 * LANES), lambda b, j, i: (0, col(b, j, i))),
            pl.BlockSpec((seq, 2 * LANES), lambda b, j, i: (0, col(b, j, i))),
        ],
        out_specs=pl.BlockSpec((tq, LANES), lambda b, j, i: (i, col(b, j, i))),
        out_shape=jax.ShapeDtypeStruct((seq, batch * MLA_HEADS * V_DIM), F32),
        compiler_params=_params("arbitrary", "arbitrary", "arbitrary"),
        name="causal_attention",
    )(q, k, v)


def _s5_discretize_kernel(lr_ref, li_ref, dt_ref, lre_ref, lie_ref, dte_ref, bre_ref, bim_ref,
                          ar_ref, ai_ref, bbr_ref, bbi_ref):
    def zoh(lr, li, dt):
        mag = jnp.exp(lr * dt)
        ar, ai = mag * jnp.cos(li * dt), mag * jnp.sin(li * dt)
        nr, ni = ar - 1.0, ai
        den = lr * lr + li * li
        return ar, ai, (nr * lr + ni * li) / den, (ni * lr - nr * li) / den

    ar, ai, _, _ = zoh(lr_ref[...], li_ref[...], dt_ref[...])
    ar_ref[...] = ar
    ai_ref[...] = ai
    _, _, zr, zi = zoh(lre_ref[...], lie_ref[...], dte_ref[...])
    br, bi = bre_ref[...], bim_ref[...]
    bbr_ref[...] = zr * br - zi * bi
    bbi_ref[...] = zr * bi + zi * br


def _s5_discretize(lambda_re, lambda_im, log_step, b_re, b_im):
    g, p = lambda_re.shape
    c = b_re.shape[-1]
    dt = jnp.broadcast_to(jnp.exp(log_step)[:, None], (g, p))
    expand = lambda a: jnp.repeat(a, c, axis=1)
    flat = lambda a: a.reshape(g, p * c)
    out = pl.pallas_call(
        _s5_discretize_kernel,
        out_shape=[jax.ShapeDtypeStruct((g, p), F32)] * 2 + [jax.ShapeDtypeStruct((g, p * c), F32)] * 2,
        name="s5_discretize",
    )(lambda_re, lambda_im, dt, expand(lambda_re), expand(lambda_im), expand(dt), flat(b_re), flat(b_im))
    ar, ai, bbr, bbi = out
    return ar, ai, bbr.reshape(g, p, c), bbi.reshape(g, p, c)


def _s5_kernel(x_ref, g_ref, win_ref, bm_ref, a_ref, cm_ref, d_ref, wglu_ref, o_ref,
               state_ref, u_ref, bu_ref, st_ref, y_ref, *, batch, nblock, half):
    @pl.when(pl.program_id(0) == 0)
    def _():
        state_ref[...] = jnp.zeros_like(state_ref)

    x = x_ref[...]
    rows, d = x.shape
    steps = rows // batch
    h = _rms(x, g_ref[...]).astype(BF16)
    u = jnp.dot(h, win_ref[...], preferred_element_type=F32)
    u_ref[...] = u
    ub = u.astype(BF16)
    for blk in range(nblock):
        sl = slice(blk * LANES, (blk + 1) * LANES)
        bu_ref[...] = jnp.dot(ub[:, sl], bm_ref[blk], preferred_element_type=F32)
        ar = a_ref[blk, 0:batch, :]
        ai = a_ref[blk, batch:2 * batch, :]

        def step(t, carry):
            sr, si = carry
            r0 = pl.multiple_of(t * batch, batch)
            br = bu_ref[pl.ds(r0, batch), 0:half]
            bi = bu_ref[pl.ds(r0, batch), half:2 * half]
            nr = ar * sr - ai * si + br
            ni = ar * si + ai * sr + bi
            st_ref[pl.ds(r0, batch), 0:half] = nr
            st_ref[pl.ds(r0, batch), half:2 * half] = ni
            return nr, ni

        sr, si = lax.fori_loop(0, steps, step,
                               (state_ref[blk, 0:batch, :], state_ref[blk, batch:2 * batch, :]),
                               unroll=4)
        state_ref[blk, 0:batch, :] = sr
        state_ref[blk, batch:2 * batch, :] = si
        y_ref[:, sl] = jnp.dot(st_ref[...].astype(BF16), cm_ref[blk], preferred_element_type=F32)
    y = y_ref[...] + d_ref[...] * u_ref[...]
    act = jax.nn.gelu(y).astype(BF16)
    z = jnp.dot(act, wglu_ref[...], preferred_element_type=F32)
    o_ref[...] = x + z[:, :d] * jax.nn.sigmoid(z[:, d:])


def _s5(x, gain, w_in, lambda_re, lambda_im, log_step, b_re, b_im, c_re, c_im, d_skip, w_glu,
        *, batch, tile_rows):
    rows, d = x.shape
    groups, nstate = lambda_re.shape
    gpb = GROUPS_PER_BLOCK
    nblock = groups // gpb
    half = gpb * nstate
    ar, ai, bbr, bbi = _s5_discretize(lambda_re, lambda_im, log_step, b_re, b_im)
    eye = jnp.eye(gpb, dtype=F32)

    def in_mat(bb):
        bb = bb.reshape(nblock, gpb, nstate, SSM_GROUP)
        return jnp.einsum("Ggpc,hg->Ghcgp", bb, eye).reshape(nblock, LANES, half)

    def out_mat(cc):
        cc = cc.reshape(nblock, gpb, SSM_GROUP, nstate)
        return jnp.einsum("Ggcp,gh->Ggphc", cc, eye).reshape(nblock, half, LANES)

    bm = jnp.concatenate([in_mat(bbr), in_mat(bbi)], axis=2).astype(BF16)
    cm = jnp.concatenate([out_mat(c_re), out_mat(-c_im)], axis=1).astype(BF16)
    rep = lambda a: jnp.broadcast_to(a.reshape(nblock, 1, half), (nblock, batch, half))
    a_mat = jnp.concatenate([rep(ar), rep(ai)], axis=1)
    tr = min(tile_rows, rows)
    return pl.pallas_call(
        functools.partial(_s5_kernel, batch=batch, nblock=nblock, half=half),
        grid=(rows // tr,),
        in_specs=[
            pl.BlockSpec((tr, d), lambda i: (i, 0)),
            _resident((1, d)),
            _resident((d, d)),
            _resident(bm.shape),
            _resident(a_mat.shape),
            _resident(cm.shape),
            _resident((1, d)),
            _resident(w_glu.shape),
        ],
        out_specs=pl.BlockSpec((tr, d), lambda i: (i, 0)),
        out_shape=jax.ShapeDtypeStruct((rows, d), F32),
        scratch_shapes=[
            pltpu.VMEM((nblock, 2 * batch, half), F32),
            pltpu.VMEM((tr, d), F32),
            pltpu.VMEM((tr, 2 * half), F32),
            pltpu.VMEM((tr, 2 * half), F32),
            pltpu.VMEM((tr, d), F32),
        ],
        compiler_params=_params("arbitrary"),
        name="s5_mixer",
    )(x, gain.reshape(1, d), w_in.astype(BF16), bm, a_mat, cm, d_skip.reshape(1, d), w_glu.astype(BF16))


def kernel(x, attn_norm, mix_w_in, cq_norm, ckv_norm, w_uq, w_ukv, q_gain, k_gain, sconv_w, mix_w_out,
           ssm_norm, ssm_w_in, lambda_re, lambda_im, log_step, b_re, b_im, c_re, c_im, d_skip, w_glu,
           ffn_norm, ffn_w_up, ffn_conv_w, ffn_w_down):
    batch, seq, d = x.shape
    depth = ffn_norm.shape[0]
    xt = x.transpose(1, 0, 2).reshape(seq * batch, d)
    tables = _rope_tables(seq, batch)
    for layer in range(depth):
        i = layer // 2
        mixer = None
        if layer % 2 == 0:
            q, k, v, conv = _mixproj(xt, attn_norm[i], mix_w_in[i], cq_norm[i], ckv_norm[i], w_uq[i],
                                     w_ukv[i], q_gain[i], k_gain[i], sconv_w[i], tables,
                                     batch=batch, tile_rows=512)
            mixer = (_attention(q, k, v, seq=seq, batch=batch, tq=512), conv, mix_w_out[i])
        else:
            xt = _s5(xt, ssm_norm[i], ssm_w_in[i], lambda_re[i], lambda_im[i], log_step[i],
                     b_re[i], b_im[i], c_re[i], c_im[i], d_skip[i], w_glu[i],
                     batch=batch, tile_rows=512)
        xt = _ffn(xt, ffn_norm[layer], ffn_w_up[layer], ffn_conv_w[layer], ffn_w_down[layer],
                  batch=batch, tile_rows=512, fc=256, mixer=mixer)
    return xt.reshape(seq, batch, d).transpose(1, 0, 2)
```

```python
import functools
import math

import jax
import jax.numpy as jnp
from jax import lax
from jax.experimental import pallas as pl
from jax.experimental.pallas import tpu as pltpu

F32 = jnp.float32
BF16 = jnp.bfloat16

EPS = 1e-6
ROPE_THETA = 10000.0
LANES = 128
MXU_TILE = 256
MLA_HEADS = 8
QK_NOPE = 64
QK_ROPE = 32
QK_DIM = QK_NOPE + QK_ROPE
V_DIM = 64
LORA = 256
CONV_CH = 512
SSM_GROUP = 16
GROUPS_PER_BLOCK = LANES // SSM_GROUP
HEAD_LANES = MLA_HEADS * LANES
VMEM_LIMIT = 56 * 1024 * 1024
ROW_TILE = 512
ATTN_TILE = 1024
FFN_CHUNK = MXU_TILE


def _params(*sem):
    return pltpu.CompilerParams(dimension_semantics=sem, vmem_limit_bytes=VMEM_LIMIT)


def _resident(shape, layer=None):
    if layer is None:
        nd = len(shape)
        return pl.BlockSpec(shape, lambda *_: (0,) * nd, pipeline_mode=pl.Buffered(1))
    nd = len(shape) - 1
    return pl.BlockSpec((None,) + tuple(shape[1:]), lambda *_: (layer,) + (0,) * nd,
                        pipeline_mode=pl.Buffered(1))


def _rms(x, gain):
    return x * lax.rsqrt(jnp.mean(x * x, axis=-1, keepdims=True) + EPS) * gain


def _dot_split(a, w, parts=2):
    rows = a.shape[0]
    step = rows // parts
    return jnp.concatenate(
        [jnp.dot(a[r:r + step], w, preferred_element_type=F32) for r in range(0, rows, step)], axis=0)


def _shift_conv(prev, cur, w):
    rows = cur.shape[0]
    step = prev.shape[0] // 2
    ext = jnp.concatenate([prev, cur], axis=0)
    return w[0:1] * ext[0:rows] + w[1:2] * ext[step:rows + step] + w[2:3] * cur


def _to_time_major(src_ref, dst_ref, batch):
    steps = src_ref.shape[1]
    for b in range(batch):
        for c in range(dst_ref.shape[0]):
            dst_ref[c, pl.ds(b, steps, stride=batch), :] = src_ref[b, :, c * LANES:(c + 1) * LANES]


def _ffn_kernel(*refs, layer, hidden, halo, batch, mix, last):
    refs = list(refs)
    x_ref = refs.pop(0)
    if mix:
        a_ref, cv_ref, wo_ref = refs.pop(0), refs.pop(0), refs.pop(0)
    g_ref, wup_ref, cw_ref, wdn_ref, o_ref, carry_ref = (refs.pop(0) for _ in range(6))
    if mix:
        ail_ref = refs.pop(0)
    if last:
        os_ref = refs.pop(0)

    @pl.when(pl.program_id(0) == 0)
    def _():
        carry_ref[...] = jnp.zeros_like(carry_ref)

    x = x_ref[...]
    rows, d = x.shape
    steps = rows // batch
    nblk = d // LANES
    if mix:
        na = a_ref.shape[1] // batch
        for b in range(batch):
            for c in range(na // LANES):
                col = b * na + c * LANES
                ail_ref[c, pl.ds(b, steps, stride=batch), :] = a_ref[:, col:col + LANES]
        attn = jnp.concatenate([ail_ref[c] for c in range(na // LANES)], axis=1).astype(BF16)
        x = (x + jnp.dot(attn, wo_ref[:na, :], preferred_element_type=F32)
             + jnp.dot(cv_ref[...], wo_ref[na:, :], preferred_element_type=F32))
    h = _rms(x, g_ref[layer:layer + 1, :]).astype(BF16)
    if last:
        for c in range(nblk):
            os_ref[c] = x[:, c * LANES:(c + 1) * LANES]
    else:
        o_ref[...] = x

    for c0 in range(0, hidden, FFN_CHUNK):
        def half(j):
            cols = slice(j * hidden + c0, j * hidden + c0 + FFN_CHUNK)
            u = _dot_split(h, wup_ref[:, cols])
            prev = carry_ref[:, cols]
            carry_ref[:, cols] = u[rows - halo:, :]
            return _shift_conv(prev, u, cw_ref[:, cols])

        gate = half(0)
        val = half(1)
        act = (gate * jax.nn.sigmoid(gate) * val).astype(BF16)
        res = jnp.dot(act, wdn_ref[c0:c0 + FFN_CHUNK, :], preferred_element_type=F32)
        if last:
            for c in range(nblk):
                os_ref[c] += res[:, c * LANES:(c + 1) * LANES]
        else:
            o_ref[...] += res
    if last:
        for b in range(batch):
            for c in range(nblk):
                o_ref[b, :, c * LANES:(c + 1) * LANES] = os_ref[c, pl.ds(b, steps, stride=batch), :]


def _ffn(x, w, layer, *, batch, mixer=None, last=False):
    rows, d = x.shape
    hidden = w["wdn"].shape[1]
    halo = 2 * batch
    tr = min(ROW_TILE, rows)
    ts = tr // batch
    row_spec = lambda n: pl.BlockSpec((tr, n), lambda i: (i, 0))
    operands = [x]
    in_specs = [row_spec(d)]
    scratch = [pltpu.VMEM((halo, 2 * hidden), F32)]
    if mixer is not None:
        attn, conv, w_out, mix_layer = mixer
        operands += [attn, conv, w_out]
        in_specs += [pl.BlockSpec((ts, attn.shape[1]), lambda i: (i, 0)), row_spec(conv.shape[1]),
                     _resident(w_out.shape, mix_layer)]
        scratch.append(pltpu.VMEM((attn.shape[1] // batch // LANES, tr, LANES), F32))
    operands += [w["norm"], w["wup"], w["cw"], w["wdn"]]
    in_specs += [_resident(w["norm"].shape), _resident(w["wup"].shape, layer),
                 _resident(w["cw"].shape, layer), _resident(w["wdn"].shape, layer)]
    if last:
        scratch.append(pltpu.VMEM((d // LANES, tr, LANES), F32))
        out_spec = pl.BlockSpec((batch, ts, d), lambda i: (0, i, 0))
        out_shape = jax.ShapeDtypeStruct((batch, rows // batch, d), F32)
    else:
        out_spec = row_spec(d)
        out_shape = jax.ShapeDtypeStruct((rows, d), F32)
    return pl.pallas_call(
        functools.partial(_ffn_kernel, layer=layer, hidden=hidden, halo=halo, batch=batch,
                          mix=mixer is not None, last=last),
        grid=(rows // tr,),
        in_specs=in_specs,
        out_specs=out_spec,
        out_shape=out_shape,
        scratch_shapes=scratch,
        compiler_params=_params("arbitrary"),
        name="conv_ffn",
    )(*operands)


def _ffn_weights(ffn_norm, ffn_w_up, ffn_conv_w, ffn_w_down):
    return dict(norm=ffn_norm, wup=ffn_w_up.astype(BF16), cw=ffn_conv_w, wdn=ffn_w_down.astype(BF16))


def _head_sumsq(x, ones_ref):
    sq = (x * x).astype(BF16)
    parts = [jnp.dot(sq[:, c:c + MXU_TILE], ones_ref[...], preferred_element_type=F32)
             for c in range(0, x.shape[1], MXU_TILE)]
    return jnp.concatenate(parts, axis=1)


def _mixproj_kernel(*refs, layer, halo, batch, first):
    refs = list(refs)
    (x_ref, g_ref, win_ref, cqg_ref, ckvg_ref, wuq_ref, wukv_ref, gains_ref, ones_ref, scw_ref,
     ct_ref, st_ref, q_ref, k_ref, v_ref, cv_ref) = (refs.pop(0) for _ in range(16))
    if first:
        xtm_ref = refs.pop(0)
    carry_ref, qs_ref, ks_ref, vs_ref = (refs.pop(0) for _ in range(4))
    if first:
        xs_ref = refs.pop(0)

    @pl.when(pl.program_id(0) == 0)
    def _():
        carry_ref[...] = jnp.zeros_like(carry_ref)

    if first:
        _to_time_major(x_ref, xs_ref, batch)
        x = jnp.concatenate([xs_ref[c] for c in range(xs_ref.shape[0])], axis=1)
        xtm_ref[...] = x
    else:
        x = x_ref[...]
    rows = x.shape[0]
    steps = rows // batch
    pick = lambda ref: ref[layer:layer + 1, :]
    h = _rms(x, pick(g_ref)).astype(BF16)
    proj = _dot_split(h, win_ref[...])
    o = 0
    c_q = proj[:, o:o + LORA]; o += LORA
    c_kv = proj[:, o:o + LORA]; o += LORA
    k_rope = proj[:, o:o + LANES]; o += LANES
    k_rope_rot = proj[:, o:o + LANES]; o += LANES
    gate_b = proj[:, o:o + CONV_CH]; o += CONV_CH
    gate_c = proj[:, o:o + CONV_CH]; o += CONV_CH
    conv_in = proj[:, o:o + CONV_CH]

    m = gate_c * conv_in
    prev = carry_ref[...]
    carry_ref[...] = m[rows - halo:, :]
    cv_ref[...] = (gate_b * _shift_conv(prev, m, scw_ref[layer])).astype(BF16)

    qq = jnp.dot(_rms(c_q, pick(cqg_ref)).astype(BF16), wuq_ref[...], preferred_element_type=F32)
    kv = jnp.dot(_rms(c_kv, pick(ckvg_ref)).astype(BF16), wukv_ref[...], preferred_element_type=F32)
    q, q_rot = qq[:, :HEAD_LANES], qq[:, HEAD_LANES:]
    kn, v = kv[:, :HEAD_LANES], kv[:, HEAD_LANES:]
    ct, st = ct_ref[...], st_ref[...]
    gains = gains_ref[layer]
    tq, sq = ct * gains[0:1], st * gains[1:2]
    tk, sk = ct * gains[2:3], st * gains[3:4]
    k_nope_gain, v_ones = gains[4:5], gains[5:6]
    inv_dim = 1.0 / QK_DIM
    q_inv = lax.rsqrt(_head_sumsq(q, ones_ref) * inv_dim + EPS)
    kr_sumsq = jnp.sum(k_rope * k_rope, axis=-1, keepdims=True)
    k_inv = lax.rsqrt((_head_sumsq(kn, ones_ref) + kr_sumsq) * inv_dim + EPS)
    kr = k_rope * tk + k_rope_rot * sk
    for hd in range(MLA_HEADS):
        sl = slice(hd * LANES, (hd + 1) * LANES)
        qs_ref[hd] = (q[:, sl] * tq + q_rot[:, sl] * sq) * q_inv[:, sl]
        ks_ref[hd] = (kn[:, sl] * k_nope_gain + kr) * k_inv[:, sl]
        vs_ref[hd] = v[:, sl] + v_ones
    for src, dst in ((qs_ref, q_ref), (ks_ref, k_ref), (vs_ref, v_ref)):
        for b in range(batch):
            for hd in range(MLA_HEADS):
                col = (b * MLA_HEADS + hd) * LANES
                dst[:, col:col + LANES] = src[hd, pl.ds(b, steps, stride=batch), :].astype(BF16)


def _mixproj(x, w, layer, tables, *, batch, first=False):
    if first:
        _, seq, d = x.shape
        rows = seq * batch
    else:
        rows, d = x.shape
    halo = 2 * batch
    tr = min(ROW_TILE, rows)
    ts = tr // batch
    row_spec = lambda n: pl.BlockSpec((tr, n), lambda i: (i, 0))
    wide_spec = pl.BlockSpec((ts, batch * HEAD_LANES), lambda i: (i, 0))
    wide_shape = jax.ShapeDtypeStruct((rows // batch, batch * HEAD_LANES), BF16)
    x_spec = pl.BlockSpec((batch, ts, d), lambda i: (0, i, 0)) if first else row_spec(d)
    out_specs = [wide_spec, wide_spec, wide_spec, row_spec(CONV_CH)]
    out_shape = [wide_shape, wide_shape, wide_shape, jax.ShapeDtypeStruct((rows, CONV_CH), BF16)]
    scratch = [pltpu.VMEM((halo, CONV_CH), F32)] + [pltpu.VMEM((MLA_HEADS, tr, LANES), F32)] * 3
    if first:
        out_specs.append(row_spec(d))
        out_shape.append(jax.ShapeDtypeStruct((rows, d), F32))
        scratch.append(pltpu.VMEM((d // LANES, tr, LANES), F32))
    return pl.pallas_call(
        functools.partial(_mixproj_kernel, layer=layer, halo=halo, batch=batch, first=first),
        grid=(rows // tr,),
        in_specs=[
            x_spec,
            _resident(w["norm"].shape),
            _resident(w["win"].shape, layer),
            _resident(w["cq_norm"].shape),
            _resident(w["ckv_norm"].shape),
            _resident(w["wuq"].shape, layer),
            _resident(w["wukv"].shape, layer),
            _resident(w["gains"].shape),
            _resident(w["ones_bd"].shape),
            _resident(w["scw"].shape),
            row_spec(LANES), row_spec(LANES),
        ],
        out_specs=out_specs,
        out_shape=out_shape,
        scratch_shapes=scratch,
        compiler_params=_params("arbitrary"),
        name="mix_proj",
    )(x, w["norm"], w["win"], w["cq_norm"], w["ckv_norm"], w["wuq"], w["wukv"], w["gains"],
      w["ones_bd"], w["scw"], *tables)


def _mixproj_weights(attn_norm, mix_w_in, cq_norm, ckv_norm, w_uq, w_ukv, q_gain, k_gain, sconv_w):
    nl, d, _ = mix_w_in.shape
    half = QK_ROPE // 2
    tail = LANES - QK_DIM
    zeros = lambda n: jnp.zeros((nl, d, n), F32)
    o1, o2 = 2 * LORA, 2 * LORA + QK_ROPE
    w_kr = mix_w_in[:, :, o1:o2]
    win = jnp.concatenate(
        [mix_w_in[:, :, :o1],
         zeros(QK_NOPE), w_kr, zeros(tail),
         zeros(QK_NOPE), -w_kr[:, :, half:], w_kr[:, :, :half], zeros(tail),
         mix_w_in[:, :, o2:]], axis=2).astype(BF16)
    wq = w_uq.reshape(nl, LORA, MLA_HEADS, QK_DIM)
    pad = lambda a: jnp.pad(a, ((0, 0), (0, 0), (0, 0), (0, LANES - a.shape[3]))).reshape(nl, LORA, HEAD_LANES)
    wq_rot = jnp.concatenate([jnp.zeros_like(wq[..., :QK_NOPE]), -wq[..., QK_NOPE + half:],
                              wq[..., QK_NOPE:QK_NOPE + half]], axis=3)
    wuq = jnp.concatenate([pad(wq), pad(wq_rot)], axis=2).astype(BF16)
    wkv = w_ukv.reshape(nl, LORA, MLA_HEADS, QK_NOPE + V_DIM)
    wukv = jnp.concatenate([pad(wkv[..., :QK_NOPE]), pad(wkv[..., QK_NOPE:])], axis=2).astype(BF16)

    def rope_gains(g):
        z = jnp.zeros((nl, tail), F32)
        return (jnp.concatenate([g, z], axis=1),
                jnp.concatenate([jnp.zeros((nl, QK_NOPE), F32), g[:, QK_NOPE + half:],
                                 g[:, QK_NOPE:QK_NOPE + half], z], axis=1))

    gq = rope_gains(q_gain * (QK_DIM ** -0.5 * math.log2(math.e)))
    gk = rope_gains(k_gain)
    k_nope_gain = jnp.concatenate([k_gain[:, :QK_NOPE], jnp.zeros((nl, LANES - QK_NOPE), F32)], axis=1)
    v_ones = jnp.broadcast_to((jnp.arange(LANES) == V_DIM).astype(F32), (nl, LANES))
    zero = jnp.zeros((nl, LANES), F32)
    gains = jnp.stack([*gq, *gk, k_nope_gain, v_ones, zero, zero], axis=1)
    lane_head = jnp.arange(MXU_TILE) // LANES
    ones_bd = (lane_head[:, None] == lane_head[None, :]).astype(BF16)
    return dict(norm=attn_norm, win=win, cq_norm=cq_norm, ckv_norm=ckv_norm, wuq=wuq, wukv=wukv,
                gains=gains, ones_bd=ones_bd, scw=sconv_w)


def _rope_tables(seq, batch):
    inv_freq = 1.0 / (ROPE_THETA ** (jnp.arange(0, QK_ROPE, 2, dtype=F32) / QK_ROPE))
    ang = jnp.arange(seq, dtype=F32)[:, None] * inv_freq[None, :]
    cos, sin = jnp.cos(ang), jnp.sin(ang)
    z = lambda n: jnp.zeros((seq, n), F32)
    tail = LANES - QK_DIM
    ct = jnp.concatenate([jnp.ones((seq, QK_NOPE), F32), cos, cos, z(tail)], axis=1)
    st = jnp.concatenate([z(QK_NOPE), sin, sin, z(tail)], axis=1)
    return tuple(jnp.repeat(t, batch, axis=0) for t in (ct, st))


def _attn_kernel(q_ref, k_ref, v_ref, o_ref, sa_ref, sb_ref, m_ref, acc_ref, *, tq):
    tk = tq // 2
    qi = pl.program_id(2)
    heads = [slice(hh * LANES, (hh + 1) * LANES) for hh in range(2)]
    nt = (((1,), (1,)), ((), ()))

    def scores(kb, dst, r0=0):
        start = pl.multiple_of(kb * tk, tk)
        for hh, sl in enumerate(heads):
            dst[hh, 0:tq - r0, :] = lax.dot_general(q_ref[r0:, sl], k_ref[pl.ds(start, tk), sl], nt,
                                                    preferred_element_type=F32)

    def update(kb, src, r0=0, diagonal=False):
        start = pl.multiple_of(kb * tk, tk)
        n = tq - r0
        for hh, sl in enumerate(heads):
            s = src[hh, 0:n, :]
            if diagonal:
                row = lax.broadcasted_iota(jnp.int32, (n, tk), 0)
                col = lax.broadcasted_iota(jnp.int32, (n, tk), 1)
                s = jnp.where(col <= row, s, -jnp.inf)
            m_old = m_ref[hh, r0:, :]
            m_new = jnp.maximum(m_old, jnp.max(s, axis=-1, keepdims=True))
            p = jnp.concatenate(
                [jnp.exp2((s[:, c:c + LANES] - m_new).astype(BF16)) for c in range(0, tk, LANES)], axis=1)
            pv = jnp.dot(p, v_ref[pl.ds(start, tk), sl], preferred_element_type=F32)
            acc_ref[hh, r0:, :] = jnp.exp2(m_old - m_new) * acc_ref[hh, r0:, :] + pv
            m_ref[hh, r0:, :] = m_new

    m_ref[...] = jnp.full(m_ref.shape, -jnp.inf, F32)
    acc_ref[...] = jnp.zeros(acc_ref.shape, F32)
    scores(0, sa_ref)

    def body(j, _):
        scores(2 * j + 1, sb_ref)
        update(2 * j, sa_ref)
        scores(2 * j + 2, sa_ref)
        update(2 * j + 1, sb_ref)
        return 0

    lax.fori_loop(0, qi, body, 0)
    scores(2 * qi + 1, sb_ref, r0=tk)
    update(2 * qi, sa_ref, diagonal=True)
    update(2 * qi + 1, sb_ref, r0=tk, diagonal=True)
    lane = lax.broadcasted_iota(jnp.int32, (1, LANES), 1)
    outs = []
    for hh in range(2):
        acc = acc_ref[hh]
        outs.append(acc / acc[:, V_DIM:V_DIM + 1])
    o_ref[...] = jnp.where(lane < V_DIM, outs[0], pltpu.roll(outs[1], V_DIM, 1))


def _attention(q, k, v, *, seq, batch):
    tq = min(ATTN_TILE, seq)
    pairs = MLA_HEADS // 2
    col = lambda b, j, i: b * pairs + j
    return pl.pallas_call(
        functools.partial(_attn_kernel, tq=tq),
        grid=(batch, pairs, seq // tq),
        in_specs=[
            pl.BlockSpec((tq, 2 * LANES), lambda b, j, i: (i, col(b, j, i))),
            pl.BlockSpec((seq, 2 * LANES), lambda b, j, i: (0, col(b, j, i))),
            pl.BlockSpec((seq, 2 * LANES), lambda b, j, i: (0, col(b, j, i))),
        ],
        out_specs=pl.BlockSpec((tq, LANES), lambda b, j, i: (i, col(b, j, i))),
        out_shape=jax.ShapeDtypeStruct((seq, batch * MLA_HEADS * V_DIM), F32),
        scratch_shapes=[
            pltpu.VMEM((2, tq, tq // 2), F32),
            pltpu.VMEM((2, tq, tq // 2), F32),
            pltpu.VMEM((2, tq, LANES), F32),
            pltpu.VMEM((2, tq, LANES), F32),
        ],
        compiler_params=_params("arbitrary", "arbitrary", "arbitrary"),
        name="causal_attention",
    )(q, k, v)


def _s5_discretize_kernel(lr_ref, li_ref, dt_ref, lre_ref, lie_ref, dte_ref, bre_ref, bim_ref,
                          ar_ref, ai_ref, bbr_ref, bbi_ref):
    def zoh(lr, li, dt):
        mag = jnp.exp(lr * dt)
        ar, ai = mag * jnp.cos(li * dt), mag * jnp.sin(li * dt)
        nr, ni = ar - 1.0, ai
        den = lr * lr + li * li
        return ar, ai, (nr * lr + ni * li) / den, (ni * lr - nr * li) / den

    ar, ai, _, _ = zoh(lr_ref[...], li_ref[...], dt_ref[...])
    ar_ref[...] = ar
    ai_ref[...] = ai
    _, _, zr, zi = zoh(lre_ref[...], lie_ref[...], dte_ref[...])
    br, bi = bre_ref[...], bim_ref[...]
    bbr_ref[...] = zr * br - zi * bi
    bbi_ref[...] = zr * bi + zi * br


def _s5_discretize(lambda_re, lambda_im, log_step, b_re, b_im):
    g, p = lambda_re.shape
    c = b_re.shape[-1]
    dt = jnp.broadcast_to(jnp.exp(log_step)[:, None], (g, p))
    expand = lambda a: jnp.repeat(a, c, axis=1)
    flat = lambda a: a.reshape(g, p * c)
    out = pl.pallas_call(
        _s5_discretize_kernel,
        out_shape=[jax.ShapeDtypeStruct((g, p), F32)] * 2 + [jax.ShapeDtypeStruct((g, p * c), F32)] * 2,
        name="s5_discretize",
    )(lambda_re, lambda_im, dt, expand(lambda_re), expand(lambda_im), expand(dt), flat(b_re), flat(b_im))
    ar, ai, bbr, bbi = out
    return ar, ai, bbr.reshape(g, p, c), bbi.reshape(g, p, c)


def _s5_kernel(x_ref, g_ref, win_ref, bm_ref, a_ref, cm_ref, d_ref, wglu_ref, o_ref,
               state_ref, u_ref, bu_ref, st_ref, y_ref, *, layer, batch, nblock, half):
    @pl.when(pl.program_id(0) == 0)
    def _():
        state_ref[...] = jnp.zeros_like(state_ref)

    x = x_ref[...]
    rows, d = x.shape
    steps = rows // batch
    h = _rms(x, g_ref[layer:layer + 1, :]).astype(BF16)
    u = _dot_split(h, win_ref[...])
    u_ref[...] = u
    ub = u.astype(BF16)
    lanes = lambda blk: slice(blk * LANES, (blk + 1) * LANES)

    def project_in(blk):
        bu_ref[blk % 2] = jnp.dot(ub[:, lanes(blk)], bm_ref[blk], preferred_element_type=F32)

    project_in(0)
    for blk in range(nblock):
        if blk + 1 < nblock:
            project_in(blk + 1)
        slot = blk % 2
        ar = a_ref[blk, 0:batch, :]
        ai = a_ref[blk, batch:2 * batch, :]
        sr = state_ref[blk, 0:batch, :]
        si = state_ref[blk, batch:2 * batch, :]
        for t in range(steps):
            r = slice(t * batch, (t + 1) * batch)
            sr, si = (ar * sr - ai * si + bu_ref[slot, r, 0:half],
                      ar * si + ai * sr + bu_ref[slot, r, half:2 * half])
            st_ref[slot, r, 0:half] = sr
            st_ref[slot, r, half:2 * half] = si
        state_ref[blk, 0:batch, :] = sr
        state_ref[blk, batch:2 * batch, :] = si
        y_ref[:, lanes(blk)] = jnp.dot(st_ref[slot].astype(BF16), cm_ref[blk], preferred_element_type=F32)
    y = y_ref[...] + d_ref[layer:layer + 1, :] * u_ref[...]
    act = jax.nn.gelu(y).astype(BF16)
    z = _dot_split(act, wglu_ref[...])
    o_ref[...] = x + z[:, :d] * jax.nn.sigmoid(z[:, d:])


def _s5(x, w, layer, *, batch):
    rows, d = x.shape
    _, nblock, _, width = w["bm"].shape
    half = width // 2
    tr = min(ROW_TILE, rows)
    return pl.pallas_call(
        functools.partial(_s5_kernel, layer=layer, batch=batch, nblock=nblock, half=half),
        grid=(rows // tr,),
        in_specs=[
            pl.BlockSpec((tr, d), lambda i: (i, 0)),
            _resident(w["norm"].shape),
            _resident(w["win"].shape, layer),
            _resident(w["bm"].shape, layer),
            _resident(w["a"].shape, layer),
            _resident(w["cm"].shape, layer),
            _resident(w["d_skip"].shape),
            _resident(w["wglu"].shape, layer),
        ],
        out_specs=pl.BlockSpec((tr, d), lambda i: (i, 0)),
        out_shape=jax.ShapeDtypeStruct((rows, d), F32),
        scratch_shapes=[
            pltpu.VMEM((nblock, 2 * batch, half), F32),
            pltpu.VMEM((tr, d), F32),
            pltpu.VMEM((2, tr, 2 * half), F32),
            pltpu.VMEM((2, tr, 2 * half), F32),
            pltpu.VMEM((tr, d), F32),
        ],
        compiler_params=_params("arbitrary"),
        name="s5_mixer",
    )(x, w["norm"], w["win"], w["bm"], w["a"], w["cm"], w["d_skip"], w["wglu"])


def _s5_weights(ssm_norm, ssm_w_in, lambda_re, lambda_im, log_step, b_re, b_im, c_re, c_im, d_skip, w_glu,
                *, batch):
    nl, groups, nstate = lambda_re.shape
    gpb = GROUPS_PER_BLOCK
    nblock = groups // gpb
    half = gpb * nstate
    merge = lambda a: a.reshape((nl * groups,) + a.shape[2:])
    ar, ai, bbr, bbi = _s5_discretize(merge(lambda_re), merge(lambda_im), merge(log_step),
                                      merge(b_re), merge(b_im))
    eye = jnp.eye(gpb, dtype=F32)

    def in_mat(bb):
        bb = bb.reshape(nl, nblock, gpb, nstate, SSM_GROUP)
        return jnp.einsum("LGgpc,hg->LGhcgp", bb, eye).reshape(nl, nblock, LANES, half)

    def out_mat(cc):
        cc = cc.reshape(nl, nblock, gpb, SSM_GROUP, nstate)
        return jnp.einsum("LGgcp,gh->LGgphc", cc, eye).reshape(nl, nblock, half, LANES)

    bm = jnp.concatenate([in_mat(bbr), in_mat(bbi)], axis=3).astype(BF16)
    cm = jnp.concatenate([out_mat(c_re), out_mat(-c_im)], axis=2).astype(BF16)
    rep = lambda a: jnp.broadcast_to(a.reshape(nl, nblock, 1, half), (nl, nblock, batch, half))
    a_mat = jnp.concatenate([rep(ar), rep(ai)], axis=2)
    return dict(norm=ssm_norm, win=ssm_w_in.astype(BF16), bm=bm, a=a_mat, cm=cm, d_skip=d_skip,
                wglu=w_glu.astype(BF16))


def kernel(x, attn_norm, mix_w_in, cq_norm, ckv_norm, w_uq, w_ukv, q_gain, k_gain, sconv_w, mix_w_out,
           ssm_norm, ssm_w_in, lambda_re, lambda_im, log_step, b_re, b_im, c_re, c_im, d_skip, w_glu,
           ffn_norm, ffn_w_up, ffn_conv_w, ffn_w_down):
    batch, seq, d = x.shape
    depth = ffn_norm.shape[0]
    tables = _rope_tables(seq, batch)
    mix_w = _mixproj_weights(attn_norm, mix_w_in, cq_norm, ckv_norm, w_uq, w_ukv, q_gain, k_gain, sconv_w)
    w_out = mix_w_out.astype(BF16)
    s5_w = _s5_weights(ssm_norm, ssm_w_in, lambda_re, lambda_im, log_step, b_re, b_im, c_re, c_im,
                       d_skip, w_glu, batch=batch)
    ffn_w = _ffn_weights(ffn_norm, ffn_w_up, ffn_conv_w, ffn_w_down)
    xt = x
    for layer in range(depth):
        i = layer // 2
        mixer = None
        if layer % 2 == 0:
            first = layer == 0
            outs = _mixproj(xt, mix_w, i, tables, batch=batch, first=first)
            q, k, v, conv = outs[:4]
            if first:
                xt = outs[4]
            mixer = (_attention(q, k, v, seq=seq, batch=batch), conv, w_out, i)
        else:
            xt = _s5(xt, s5_w, i, batch=batch)
        xt = _ffn(xt, ffn_w, layer, batch=batch, mixer=mixer, last=layer == depth - 1)
    return xt
```

```python
import functools
import math

import jax
import jax.numpy as jnp
from jax import lax
from jax.experimental import pallas as pl
from jax.experimental.pallas import tpu as pltpu

F32 = jnp.float32
BF16 = jnp.bfloat16

EPS = 1e-6
ROPE_THETA = 10000.0
LANES = 128
MXU_TILE = 256
MLA_HEADS = 8
QK_NOPE = 64
QK_ROPE = 32
QK_DIM = QK_NOPE + QK_ROPE
V_DIM = 64
LORA = 256
CONV_CH = 512
SSM_GROUP = 16
GROUPS_PER_BLOCK = LANES // SSM_GROUP
HEAD_LANES = MLA_HEADS * LANES
VMEM_LIMIT = 56 * 1024 * 1024
ROW_TILE = 512
ATTN_TILE = 1024
FFN_CHUNK = MXU_TILE


def _params(*sem):
    return pltpu.CompilerParams(dimension_semantics=sem, vmem_limit_bytes=VMEM_LIMIT)


def _resident(shape, layer=None):
    if layer is None:
        nd = len(shape)
        return pl.BlockSpec(shape, lambda *_: (0,) * nd, pipeline_mode=pl.Buffered(1))
    nd = len(shape) - 1
    return pl.BlockSpec((None,) + tuple(shape[1:]), lambda *_: (layer,) + (0,) * nd,
                        pipeline_mode=pl.Buffered(1))


def _rms(x, gain):
    return x * lax.rsqrt(jnp.mean(x * x, axis=-1, keepdims=True) + EPS) * gain


def _dot_split(a, w, parts=2):
    rows = a.shape[0]
    step = rows // parts
    return jnp.concatenate(
        [jnp.dot(a[r:r + step], w, preferred_element_type=F32) for r in range(0, rows, step)], axis=0)


def _shift_conv(prev, cur, w):
    rows = cur.shape[0]
    step = prev.shape[0] // 2
    ext = jnp.concatenate([prev, cur], axis=0)
    return w[0:1] * ext[0:rows] + w[1:2] * ext[step:rows + step] + w[2:3] * cur


def _to_time_major(src_ref, dst_ref, batch):
    steps = src_ref.shape[1]
    for b in range(batch):
        for c in range(dst_ref.shape[0]):
            dst_ref[c, pl.ds(b, steps, stride=batch), :] = src_ref[b, :, c * LANES:(c + 1) * LANES]


def _ffn_kernel(*refs, layer, hidden, halo, batch, mix, last):
    refs = list(refs)
    x_ref = refs.pop(0)
    if mix:
        a_ref, cv_ref, wo_ref = refs.pop(0), refs.pop(0), refs.pop(0)
    g_ref, wup_ref, cw_ref, wdn_ref, o_ref, carry_ref = (refs.pop(0) for _ in range(6))
    if mix:
        ail_ref = refs.pop(0)
    if last:
        os_ref = refs.pop(0)

    @pl.when(pl.program_id(0) == 0)
    def _():
        carry_ref[...] = jnp.zeros_like(carry_ref)

    x = x_ref[...]
    rows, d = x.shape
    steps = rows // batch
    nblk = d // LANES
    if mix:
        na = a_ref.shape[1] // batch
        for b in range(batch):
            for c in range(na // LANES):
                col = b * na + c * LANES
                ail_ref[c, pl.ds(b, steps, stride=batch), :] = a_ref[:, col:col + LANES]
        attn = jnp.concatenate([ail_ref[c] for c in range(na // LANES)], axis=1).astype(BF16)
        x = (x + jnp.dot(attn, wo_ref[:na, :], preferred_element_type=F32)
             + jnp.dot(cv_ref[...], wo_ref[na:, :], preferred_element_type=F32))
    h = _rms(x, g_ref[layer:layer + 1, :]).astype(BF16)
    if last:
        for c in range(nblk):
            os_ref[c] = x[:, c * LANES:(c + 1) * LANES]
    else:
        o_ref[...] = x

    def up_project(c0):
        return tuple(_dot_split(h, wup_ref[:, j * hidden + c0:j * hidden + c0 + FFN_CHUNK]) for j in range(2))

    def conv(u, j, c0):
        cols = slice(j * hidden + c0, j * hidden + c0 + FFN_CHUNK)
        prev = carry_ref[:, cols]
        carry_ref[:, cols] = u[rows - halo:, :]
        return _shift_conv(prev, u, cw_ref[:, cols])

    chunks = list(range(0, hidden, FFN_CHUNK))
    u_next = up_project(chunks[0])
    for n, c0 in enumerate(chunks):
        u_gate, u_val = u_next
        if n + 1 < len(chunks):
            u_next = up_project(chunks[n + 1])
        gate = conv(u_gate, 0, c0)
        val = conv(u_val, 1, c0)
        act = (gate * jax.nn.sigmoid(gate) * val).astype(BF16)
        res = jnp.dot(act, wdn_ref[c0:c0 + FFN_CHUNK, :], preferred_element_type=F32)
        if last:
            for c in range(nblk):
                os_ref[c] += res[:, c * LANES:(c + 1) * LANES]
        else:
            o_ref[...] += res
    if last:
        for b in range(batch):
            for c in range(nblk):
                o_ref[b, :, c * LANES:(c + 1) * LANES] = os_ref[c, pl.ds(b, steps, stride=batch), :]


def _ffn(x, w, layer, *, batch, mixer=None, last=False):
    rows, d = x.shape
    hidden = w["wdn"].shape[1]
    halo = 2 * batch
    tr = min(ROW_TILE, rows)
    ts = tr // batch
    row_spec = lambda n: pl.BlockSpec((tr, n), lambda i: (i, 0))
    operands = [x]
    in_specs = [row_spec(d)]
    scratch = [pltpu.VMEM((halo, 2 * hidden), F32)]
    if mixer is not None:
        attn, conv, w_out, mix_layer = mixer
        operands += [attn, conv, w_out]
        in_specs += [pl.BlockSpec((ts, attn.shape[1]), lambda i: (i, 0)), row_spec(conv.shape[1]),
                     _resident(w_out.shape, mix_layer)]
        scratch.append(pltpu.VMEM((attn.shape[1] // batch // LANES, tr, LANES), F32))
    operands += [w["norm"], w["wup"], w["cw"], w["wdn"]]
    in_specs += [_resident(w["norm"].shape), _resident(w["wup"].shape, layer),
                 _resident(w["cw"].shape, layer), _resident(w["wdn"].shape, layer)]
    if last:
        scratch.append(pltpu.VMEM((d // LANES, tr, LANES), F32))
        out_spec = pl.BlockSpec((batch, ts, d), lambda i: (0, i, 0))
        out_shape = jax.ShapeDtypeStruct((batch, rows // batch, d), F32)
    else:
        out_spec = row_spec(d)
        out_shape = jax.ShapeDtypeStruct((rows, d), F32)
    return pl.pallas_call(
        functools.partial(_ffn_kernel, layer=layer, hidden=hidden, halo=halo, batch=batch,
                          mix=mixer is not None, last=last),
        grid=(rows // tr,),
        in_specs=in_specs,
        out_specs=out_spec,
        out_shape=out_shape,
        scratch_shapes=scratch,
        compiler_params=_params("arbitrary"),
        name="conv_ffn",
    )(*operands)


def _ffn_weights(ffn_norm, ffn_w_up, ffn_conv_w, ffn_w_down):
    return dict(norm=ffn_norm, wup=ffn_w_up.astype(BF16), cw=ffn_conv_w, wdn=ffn_w_down.astype(BF16))


def _head_sumsq(x, ones_ref):
    sq = (x * x).astype(BF16)
    parts = [jnp.dot(sq[:, c:c + MXU_TILE], ones_ref[...], preferred_element_type=F32)
             for c in range(0, x.shape[1], MXU_TILE)]
    return jnp.concatenate(parts, axis=1)


def _mixproj_kernel(*refs, layer, halo, batch, first):
    refs = list(refs)
    (x_ref, g_ref, win_ref, cqg_ref, ckvg_ref, wuq_ref, wukv_ref, gains_ref, ones_ref, scw_ref,
     ct_ref, st_ref, q_ref, k_ref, v_ref, cv_ref) = (refs.pop(0) for _ in range(16))
    if first:
        xtm_ref = refs.pop(0)
    carry_ref, qs_ref, ks_ref, vs_ref = (refs.pop(0) for _ in range(4))
    if first:
        xs_ref = refs.pop(0)

    @pl.when(pl.program_id(0) == 0)
    def _():
        carry_ref[...] = jnp.zeros_like(carry_ref)

    if first:
        _to_time_major(x_ref, xs_ref, batch)
        x = jnp.concatenate([xs_ref[c] for c in range(xs_ref.shape[0])], axis=1)
        xtm_ref[...] = x
    else:
        x = x_ref[...]
    rows = x.shape[0]
    steps = rows // batch
    pick = lambda ref: ref[layer:layer + 1, :]
    h = _rms(x, pick(g_ref)).astype(BF16)
    proj = _dot_split(h, win_ref[...])
    o = 0
    c_q = proj[:, o:o + LORA]; o += LORA
    c_kv = proj[:, o:o + LORA]; o += LORA
    k_rope = proj[:, o:o + LANES]; o += LANES
    k_rope_rot = proj[:, o:o + LANES]; o += LANES
    gate_b = proj[:, o:o + CONV_CH]; o += CONV_CH
    gate_c = proj[:, o:o + CONV_CH]; o += CONV_CH
    conv_in = proj[:, o:o + CONV_CH]

    m = gate_c * conv_in
    prev = carry_ref[...]
    carry_ref[...] = m[rows - halo:, :]
    cv_ref[...] = (gate_b * _shift_conv(prev, m, scw_ref[layer])).astype(BF16)

    qq = jnp.dot(_rms(c_q, pick(cqg_ref)).astype(BF16), wuq_ref[...], preferred_element_type=F32)
    kv = jnp.dot(_rms(c_kv, pick(ckvg_ref)).astype(BF16), wukv_ref[...], preferred_element_type=F32)
    q, q_rot = qq[:, :HEAD_LANES], qq[:, HEAD_LANES:]
    kn, v = kv[:, :HEAD_LANES], kv[:, HEAD_LANES:]
    ct, st = ct_ref[...], st_ref[...]
    gains = gains_ref[layer]
    tq, sq = ct * gains[0:1], st * gains[1:2]
    tk, sk = ct * gains[2:3], st * gains[3:4]
    k_nope_gain, v_ones = gains[4:5], gains[5:6]
    inv_dim = 1.0 / QK_DIM
    q_inv = lax.rsqrt(_head_sumsq(q, ones_ref) * inv_dim + EPS)
    kr_sumsq = jnp.sum(k_rope * k_rope, axis=-1, keepdims=True)
    k_inv = lax.rsqrt((_head_sumsq(kn, ones_ref) + kr_sumsq) * inv_dim + EPS)
    kr = k_rope * tk + k_rope_rot * sk
    for hd in range(MLA_HEADS):
        sl = slice(hd * LANES, (hd + 1) * LANES)
        qs_ref[hd] = (q[:, sl] * tq + q_rot[:, sl] * sq) * q_inv[:, sl]
        ks_ref[hd] = (kn[:, sl] * k_nope_gain + kr) * k_inv[:, sl]
        vs_ref[hd] = v[:, sl] + v_ones
    for src, dst in ((qs_ref, q_ref), (ks_ref, k_ref), (vs_ref, v_ref)):
        for b in range(batch):
            for hd in range(MLA_HEADS):
                col = (b * MLA_HEADS + hd) * LANES
                dst[:, col:col + LANES] = src[hd, pl.ds(b, steps, stride=batch), :].astype(BF16)


def _mixproj(x, w, layer, tables, *, batch, first=False):
    if first:
        _, seq, d = x.shape
        rows = seq * batch
    else:
        rows, d = x.shape
    halo = 2 * batch
    tr = min(ROW_TILE, rows)
    ts = tr // batch
    row_spec = lambda n: pl.BlockSpec((tr, n), lambda i: (i, 0))
    wide_spec = pl.BlockSpec((ts, batch * HEAD_LANES), lambda i: (i, 0))
    wide_shape = jax.ShapeDtypeStruct((rows // batch, batch * HEAD_LANES), BF16)
    x_spec = pl.BlockSpec((batch, ts, d), lambda i: (0, i, 0)) if first else row_spec(d)
    out_specs = [wide_spec, wide_spec, wide_spec, row_spec(CONV_CH)]
    out_shape = [wide_shape, wide_shape, wide_shape, jax.ShapeDtypeStruct((rows, CONV_CH), BF16)]
    scratch = [pltpu.VMEM((halo, CONV_CH), F32)] + [pltpu.VMEM((MLA_HEADS, tr, LANES), F32)] * 3
    if first:
        out_specs.append(row_spec(d))
        out_shape.append(jax.ShapeDtypeStruct((rows, d), F32))
        scratch.append(pltpu.VMEM((d // LANES, tr, LANES), F32))
    return pl.pallas_call(
        functools.partial(_mixproj_kernel, layer=layer, halo=halo, batch=batch, first=first),
        grid=(rows // tr,),
        in_specs=[
            x_spec,
            _resident(w["norm"].shape),
            _resident(w["win"].shape, layer),
            _resident(w["cq_norm"].shape),
            _resident(w["ckv_norm"].shape),
            _resident(w["wuq"].shape, layer),
            _resident(w["wukv"].shape, layer),
            _resident(w["gains"].shape),
            _resident(w["ones_bd"].shape),
            _resident(w["scw"].shape),
            row_spec(LANES), row_spec(LANES),
        ],
        out_specs=out_specs,
        out_shape=out_shape,
        scratch_shapes=scratch,
        compiler_params=_params("arbitrary"),
        name="mix_proj",
    )(x, w["norm"], w["win"], w["cq_norm"], w["ckv_norm"], w["wuq"], w["wukv"], w["gains"],
      w["ones_bd"], w["scw"], *tables)


def _mixproj_weights(attn_norm, mix_w_in, cq_norm, ckv_norm, w_uq, w_ukv, q_gain, k_gain, sconv_w):
    nl, d, _ = mix_w_in.shape
    half = QK_ROPE // 2
    tail = LANES - QK_DIM
    zeros = lambda n: jnp.zeros((nl, d, n), F32)
    o1, o2 = 2 * LORA, 2 * LORA + QK_ROPE
    w_kr = mix_w_in[:, :, o1:o2]
    win = jnp.concatenate(
        [mix_w_in[:, :, :o1],
         zeros(QK_NOPE), w_kr, zeros(tail),
         zeros(QK_NOPE), -w_kr[:, :, half:], w_kr[:, :, :half], zeros(tail),
         mix_w_in[:, :, o2:]], axis=2).astype(BF16)
    wq = w_uq.reshape(nl, LORA, MLA_HEADS, QK_DIM)
    pad = lambda a: jnp.pad(a, ((0, 0), (0, 0), (0, 0), (0, LANES - a.shape[3]))).reshape(nl, LORA, HEAD_LANES)
    wq_rot = jnp.concatenate([jnp.zeros_like(wq[..., :QK_NOPE]), -wq[..., QK_NOPE + half:],
                              wq[..., QK_NOPE:QK_NOPE + half]], axis=3)
    wuq = jnp.concatenate([pad(wq), pad(wq_rot)], axis=2).astype(BF16)
    wkv = w_ukv.reshape(nl, LORA, MLA_HEADS, QK_NOPE + V_DIM)
    wukv = jnp.concatenate([pad(wkv[..., :QK_NOPE]), pad(wkv[..., QK_NOPE:])], axis=2).astype(BF16)

    def rope_gains(g):
        z = jnp.zeros((nl, tail), F32)
        return (jnp.concatenate([g, z], axis=1),
                jnp.concatenate([jnp.zeros((nl, QK_NOPE), F32), g[:, QK_NOPE + half:],
                                 g[:, QK_NOPE:QK_NOPE + half], z], axis=1))

    gq = rope_gains(q_gain * (QK_DIM ** -0.5 * math.log2(math.e)))
    gk = rope_gains(k_gain)
    k_nope_gain = jnp.concatenate([k_gain[:, :QK_NOPE], jnp.zeros((nl, LANES - QK_NOPE), F32)], axis=1)
    v_ones = jnp.broadcast_to((jnp.arange(LANES) == V_DIM).astype(F32), (nl, LANES))
    zero = jnp.zeros((nl, LANES), F32)
    gains = jnp.stack([*gq, *gk, k_nope_gain, v_ones, zero, zero], axis=1)
    lane_head = jnp.arange(MXU_TILE) // LANES
    ones_bd = (lane_head[:, None] == lane_head[None, :]).astype(BF16)
    return dict(norm=attn_norm, win=win, cq_norm=cq_norm, ckv_norm=ckv_norm, wuq=wuq, wukv=wukv,
                gains=gains, ones_bd=ones_bd, scw=sconv_w)


def _rope_tables(seq, batch):
    inv_freq = 1.0 / (ROPE_THETA ** (jnp.arange(0, QK_ROPE, 2, dtype=F32) / QK_ROPE))
    ang = jnp.arange(seq, dtype=F32)[:, None] * inv_freq[None, :]
    cos, sin = jnp.cos(ang), jnp.sin(ang)
    z = lambda n: jnp.zeros((seq, n), F32)
    tail = LANES - QK_DIM
    ct = jnp.concatenate([jnp.ones((seq, QK_NOPE), F32), cos, cos, z(tail)], axis=1)
    st = jnp.concatenate([z(QK_NOPE), sin, sin, z(tail)], axis=1)
    return tuple(jnp.repeat(t, batch, axis=0) for t in (ct, st))


def _attn_kernel(q_ref, k_ref, v_ref, o_ref, sa_ref, sb_ref, m_ref, acc_ref, *, tq):
    tk = tq // 2
    qi = pl.program_id(2)
    heads = [slice(hh * LANES, (hh + 1) * LANES) for hh in range(2)]
    nt = (((1,), (1,)), ((), ()))

    def scores(kb, dst, r0=0):
        start = pl.multiple_of(kb * tk, tk)
        for hh, sl in enumerate(heads):
            dst[hh, 0:tq - r0, :] = lax.dot_general(q_ref[r0:, sl], k_ref[pl.ds(start, tk), sl], nt,
                                                    preferred_element_type=F32)

    def update(kb, src, r0=0, diagonal=False):
        start = pl.multiple_of(kb * tk, tk)
        n = tq - r0
        for hh, sl in enumerate(heads):
            s = src[hh, 0:n, :]
            if diagonal:
                row = lax.broadcasted_iota(jnp.int32, (n, tk), 0)
                col = lax.broadcasted_iota(jnp.int32, (n, tk), 1)
                s = jnp.where(col <= row, s, -jnp.inf)
            m_old = m_ref[hh, r0:, :]
            m_new = jnp.maximum(m_old, jnp.max(s, axis=-1, keepdims=True))
            p = jnp.concatenate(
                [jnp.exp2((s[:, c:c + LANES] - m_new).astype(BF16)) for c in range(0, tk, LANES)], axis=1)
            pv = jnp.dot(p, v_ref[pl.ds(start, tk), sl], preferred_element_type=F32)
            acc_ref[hh, r0:, :] = jnp.exp2(m_old - m_new) * acc_ref[hh, r0:, :] + pv
            m_ref[hh, r0:, :] = m_new

    m_ref[...] = jnp.full(m_ref.shape, -jnp.inf, F32)
    acc_ref[...] = jnp.zeros(acc_ref.shape, F32)
    scores(0, sa_ref)

    def body(j, _):
        scores(2 * j + 1, sb_ref)
        update(2 * j, sa_ref)
        scores(2 * j + 2, sa_ref)
        update(2 * j + 1, sb_ref)
        return 0

    lax.fori_loop(0, qi, body, 0)
    scores(2 * qi + 1, sb_ref, r0=tk)
    update(2 * qi, sa_ref, diagonal=True)
    update(2 * qi + 1, sb_ref, r0=tk, diagonal=True)
    lane = lax.broadcasted_iota(jnp.int32, (1, LANES), 1)
    outs = []
    for hh in range(2):
        acc = acc_ref[hh]
        outs.append(acc / acc[:, V_DIM:V_DIM + 1])
    o_ref[...] = jnp.where(lane < V_DIM, outs[0], pltpu.roll(outs[1], V_DIM, 1))


def _attention(q, k, v, *, seq, batch):
    tq = min(ATTN_TILE, seq)
    pairs = MLA_HEADS // 2
    col = lambda b, j, i: b * pairs + j
    return pl.pallas_call(
        functools.partial(_attn_kernel, tq=tq),
        grid=(batch, pairs, seq // tq),
        in_specs=[
            pl.BlockSpec((tq, 2 * LANES), lambda b, j, i: (i, col(b, j, i))),
            pl.BlockSpec((seq, 2 * LANES), lambda b, j, i: (0, col(b, j, i))),
            pl.BlockSpec((seq, 2 * LANES), lambda b, j, i: (0, col(b, j, i))),
        ],
        out_specs=pl.BlockSpec((tq, LANES), lambda b, j, i: (i, col(b, j, i))),
        out_shape=jax.ShapeDtypeStruct((seq, batch * MLA_HEADS * V_DIM), F32),
        scratch_shapes=[
            pltpu.VMEM((2, tq, tq // 2), F32),
            pltpu.VMEM((2, tq, tq // 2), F32),
            pltpu.VMEM((2, tq, LANES), F32),
            pltpu.VMEM((2, tq, LANES), F32),
        ],
        compiler_params=_params("arbitrary", "arbitrary", "arbitrary"),
        name="causal_attention",
    )(q, k, v)


def _s5_discretize_kernel(lr_ref, li_ref, dt_ref, lre_ref, lie_ref, dte_ref, bre_ref, bim_ref,
                          ar_ref, ai_ref, bbr_ref, bbi_ref):
    def zoh(lr, li, dt):
        mag = jnp.exp(lr * dt)
        ar, ai = mag * jnp.cos(li * dt), mag * jnp.sin(li * dt)
        nr, ni = ar - 1.0, ai
        den = lr * lr + li * li
        return ar, ai, (nr * lr + ni * li) / den, (ni * lr - nr * li) / den

    ar, ai, _, _ = zoh(lr_ref[...], li_ref[...], dt_ref[...])
    ar_ref[...] = ar
    ai_ref[...] = ai
    _, _, zr, zi = zoh(lre_ref[...], lie_ref[...], dte_ref[...])
    br, bi = bre_ref[...], bim_ref[...]
    bbr_ref[...] = zr * br - zi * bi
    bbi_ref[...] = zr * bi + zi * br


def _s5_discretize(lambda_re, lambda_im, log_step, b_re, b_im):
    g, p = lambda_re.shape
    c = b_re.shape[-1]
    dt = jnp.broadcast_to(jnp.exp(log_step)[:, None], (g, p))
    expand = lambda a: jnp.repeat(a, c, axis=1)
    flat = lambda a: a.reshape(g, p * c)
    out = pl.pallas_call(
        _s5_discretize_kernel,
        out_shape=[jax.ShapeDtypeStruct((g, p), F32)] * 2 + [jax.ShapeDtypeStruct((g, p * c), F32)] * 2,
        name="s5_discretize",
    )(lambda_re, lambda_im, dt, expand(lambda_re), expand(lambda_im), expand(dt), flat(b_re), flat(b_im))
    ar, ai, bbr, bbi = out
    return ar, ai, bbr.reshape(g, p, c), bbi.reshape(g, p, c)


def _s5_kernel(x_ref, g_ref, win_ref, bm_ref, a_ref, cm_ref, d_ref, wglu_ref, o_ref,
               state_ref, u_ref, bu_ref, st_ref, y_ref, *, layer, batch, nblock, half):
    @pl.when(pl.program_id(0) == 0)
    def _():
        state_ref[...] = jnp.zeros_like(state_ref)

    x = x_ref[...]
    rows, d = x.shape
    steps = rows // batch
    h = _rms(x, g_ref[layer:layer + 1, :]).astype(BF16)
    u = _dot_split(h, win_ref[...])
    u_ref[...] = u
    ub = u.astype(BF16)
    lanes = lambda blk: slice(blk * LANES, (blk + 1) * LANES)

    def project_in(blk):
        bu_ref[blk % 2] = _dot_split(ub[:, lanes(blk)], bm_ref[blk])

    def project_out(blk):
        y_ref[:, lanes(blk)] = _dot_split(st_ref[blk % 2].astype(BF16), cm_ref[blk])

    project_in(0)
    for blk in range(nblock):
        if blk + 1 < nblock:
            project_in(blk + 1)
        if blk >= 1:
            project_out(blk - 1)
        slot = blk % 2
        ar = a_ref[blk, 0:batch, :]
        ai = a_ref[blk, batch:2 * batch, :]
        sr = state_ref[blk, 0:batch, :]
        si = state_ref[blk, batch:2 * batch, :]
        for t in range(steps):
            r = slice(t * batch, (t + 1) * batch)
            sr, si = (ar * sr - ai * si + bu_ref[slot, r, 0:half],
                      ar * si + ai * sr + bu_ref[slot, r, half:2 * half])
            st_ref[slot, r, 0:half] = sr
            st_ref[slot, r, half:2 * half] = si
        state_ref[blk, 0:batch, :] = sr
        state_ref[blk, batch:2 * batch, :] = si
    project_out(nblock - 1)
    y = y_ref[...] + d_ref[layer:layer + 1, :] * u_ref[...]
    act = jax.nn.gelu(y).astype(BF16)
    z = _dot_split(act, wglu_ref[...])
    o_ref[...] = x + z[:, :d] * jax.nn.sigmoid(z[:, d:])


def _s5(x, w, layer, *, batch):
    rows, d = x.shape
    _, nblock, _, width = w["bm"].shape
    half = width // 2
    tr = min(ROW_TILE, rows)
    return pl.pallas_call(
        functools.partial(_s5_kernel, layer=layer, batch=batch, nblock=nblock, half=half),
        grid=(rows // tr,),
        in_specs=[
            pl.BlockSpec((tr, d), lambda i: (i, 0)),
            _resident(w["norm"].shape),
            _resident(w["win"].shape, layer),
            _resident(w["bm"].shape, layer),
            _resident(w["a"].shape, layer),
            _resident(w["cm"].shape, layer),
            _resident(w["d_skip"].shape),
            _resident(w["wglu"].shape, layer),
        ],
        out_specs=pl.BlockSpec((tr, d), lambda i: (i, 0)),
        out_shape=jax.ShapeDtypeStruct((rows, d), F32),
        scratch_shapes=[
            pltpu.VMEM((nblock, 2 * batch, half), F32),
            pltpu.VMEM((tr, d), F32),
            pltpu.VMEM((2, tr, 2 * half), F32),
            pltpu.VMEM((2, tr, 2 * half), F32),
            pltpu.VMEM((tr, d), F32),
        ],
        compiler_params=_params("arbitrary"),
        name="s5_mixer",
    )(x, w["norm"], w["win"], w["bm"], w["a"], w["cm"], w["d_skip"], w["wglu"])


def _s5_weights(ssm_norm, ssm_w_in, lambda_re, lambda_im, log_step, b_re, b_im, c_re, c_im, d_skip, w_glu,
                *, batch):
    nl, groups, nstate = lambda_re.shape
    gpb = GROUPS_PER_BLOCK
    nblock = groups // gpb
    half = gpb * nstate
    merge = lambda a: a.reshape((nl * groups,) + a.shape[2:])
    ar, ai, bbr, bbi = _s5_discretize(merge(lambda_re), merge(lambda_im), merge(log_step),
                                      merge(b_re), merge(b_im))
    eye = jnp.eye(gpb, dtype=F32)

    def in_mat(bb):
        bb = bb.reshape(nl, nblock, gpb, nstate, SSM_GROUP)
        return jnp.einsum("LGgpc,hg->LGhcgp", bb, eye).reshape(nl, nblock, LANES, half)

    def out_mat(cc):
        cc = cc.reshape(nl, nblock, gpb, SSM_GROUP, nstate)
        return jnp.einsum("LGgcp,gh->LGgphc", cc, eye).reshape(nl, nblock, half, LANES)

    bm = jnp.concatenate([in_mat(bbr), in_mat(bbi)], axis=3).astype(BF16)
    cm = jnp.concatenate([out_mat(c_re), out_mat(-c_im)], axis=2).astype(BF16)
    rep = lambda a: jnp.broadcast_to(a.reshape(nl, nblock, 1, half), (nl, nblock, batch, half))
    a_mat = jnp.concatenate([rep(ar), rep(ai)], axis=2)
    return dict(norm=ssm_norm, win=ssm_w_in.astype(BF16), bm=bm, a=a_mat, cm=cm, d_skip=d_skip,
                wglu=w_glu.astype(BF16))


def kernel(x, attn_norm, mix_w_in, cq_norm, ckv_norm, w_uq, w_ukv, q_gain, k_gain, sconv_w, mix_w_out,
           ssm_norm, ssm_w_in, lambda_re, lambda_im, log_step, b_re, b_im, c_re, c_im, d_skip, w_glu,
           ffn_norm, ffn_w_up, ffn_conv_w, ffn_w_down):
    batch, seq, d = x.shape
    depth = ffn_norm.shape[0]
    tables = _rope_tables(seq, batch)
    mix_w = _mixproj_weights(attn_norm, mix_w_in, cq_norm, ckv_norm, w_uq, w_ukv, q_gain, k_gain, sconv_w)
    w_out = mix_w_out.astype(BF16)
    s5_w = _s5_weights(ssm_norm, ssm_w_in, lambda_re, lambda_im, log_step, b_re, b_im, c_re, c_im,
                       d_skip, w_glu, batch=batch)
    ffn_w = _ffn_weights(ffn_norm, ffn_w_up, ffn_conv_w, ffn_w_down)
    xt = x
    for layer in range(depth):
        i = layer // 2
        mixer = None
        if layer % 2 == 0:
            first = layer == 0
            outs = _mixproj(xt, mix_w, i, tables, batch=batch, first=first)
            q, k, v, conv = outs[:4]
            if first:
                xt = outs[4]
            mixer = (_attention(q, k, v, seq=seq, batch=batch), conv, w_out, i)
        else:
            xt = _s5(xt, s5_w, i, batch=batch)
        xt = _ffn(xt, ffn_w, layer, batch=batch, mixer=mixer, last=layer == depth - 1)
    return xt
```

```python
import functools
import math

import jax
import jax.numpy as jnp
from jax import lax
from jax.experimental import pallas as pl
from jax.experimental.pallas import tpu as pltpu

F32 = jnp.float32
BF16 = jnp.bfloat16

EPS = 1e-6
ROPE_THETA = 10000.0
LANES = 128
MXU_TILE = 256
MLA_HEADS = 8
QK_NOPE = 64
QK_ROPE = 32
QK_DIM = QK_NOPE + QK_ROPE
V_DIM = 64
LORA = 256
CONV_CH = 512
SSM_GROUP = 16
GROUPS_PER_BLOCK = LANES // SSM_GROUP
HEAD_LANES = MLA_HEADS * LANES
VMEM_LIMIT = 56 * 1024 * 1024
ROW_TILE = 512
FFN_ROW_TILE = 1024
ATTN_TILE = 1024
FFN_CHUNK = MXU_TILE


def _params(*sem):
    return pltpu.CompilerParams(dimension_semantics=sem, vmem_limit_bytes=VMEM_LIMIT)


def _resident(shape, layer=None):
    if layer is None:
        nd = len(shape)
        return pl.BlockSpec(shape, lambda *_: (0,) * nd, pipeline_mode=pl.Buffered(1))
    nd = len(shape) - 1
    return pl.BlockSpec((None,) + tuple(shape[1:]), lambda *_: (layer,) + (0,) * nd,
                        pipeline_mode=pl.Buffered(1))


def _rms(x, gain):
    return x * lax.rsqrt(jnp.mean(x * x, axis=-1, keepdims=True) + EPS) * gain


def _dot_split(a, w, parts=2):
    rows = a.shape[0]
    step = rows // parts
    return jnp.concatenate(
        [jnp.dot(a[r:r + step], w, preferred_element_type=F32) for r in range(0, rows, step)], axis=0)


def _shift_conv(prev, cur, w):
    rows = cur.shape[0]
    step = prev.shape[0] // 2
    ext = jnp.concatenate([prev, cur], axis=0)
    return w[0:1] * ext[0:rows] + w[1:2] * ext[step:rows + step] + w[2:3] * cur


def _to_time_major(src_ref, dst_ref, batch):
    steps = src_ref.shape[1]
    for b in range(batch):
        for c in range(dst_ref.shape[0]):
            dst_ref[c, pl.ds(b, steps, stride=batch), :] = src_ref[b, :, c * LANES:(c + 1) * LANES]


def _ffn_kernel(*refs, layer, hidden, halo, batch, mix, last):
    refs = list(refs)
    x_ref = refs.pop(0)
    if mix:
        a_ref, cv_ref, wo_ref = refs.pop(0), refs.pop(0), refs.pop(0)
    g_ref, wup_ref, cw_ref, wdn_ref, o_ref, carry_ref = (refs.pop(0) for _ in range(6))
    if mix:
        ail_ref = refs.pop(0)
    if last:
        os_ref = refs.pop(0)

    @pl.when(pl.program_id(0) == 0)
    def _():
        carry_ref[...] = jnp.zeros_like(carry_ref)

    x = x_ref[...]
    rows, d = x.shape
    steps = rows // batch
    nblk = d // LANES
    if mix:
        npair = a_ref.shape[1]
        na = npair * LANES
        for b in range(batch):
            for c in range(npair):
                ail_ref[c, pl.ds(b, steps, stride=batch), :] = a_ref[b, c]
        attn = jnp.concatenate([ail_ref[c] for c in range(npair)], axis=1).astype(BF16)
        x = (x + jnp.dot(attn, wo_ref[:na, :], preferred_element_type=F32)
             + jnp.dot(cv_ref[...], wo_ref[na:, :], preferred_element_type=F32))
    h = _rms(x, g_ref[layer:layer + 1, :]).astype(BF16)
    if last:
        for c in range(nblk):
            os_ref[c] = x[:, c * LANES:(c + 1) * LANES]
    else:
        o_ref[...] = x

    def up_project(c0):
        return tuple(_dot_split(h, wup_ref[:, j * hidden + c0:j * hidden + c0 + FFN_CHUNK]) for j in range(2))

    def conv(u, j, c0):
        cols = slice(j * hidden + c0, j * hidden + c0 + FFN_CHUNK)
        prev = carry_ref[:, cols]
        carry_ref[:, cols] = u[rows - halo:, :]
        return _shift_conv(prev, u, cw_ref[:, cols])

    chunks = list(range(0, hidden, FFN_CHUNK))
    u_next = up_project(chunks[0])
    for n, c0 in enumerate(chunks):
        u_gate, u_val = u_next
        if n + 1 < len(chunks):
            u_next = up_project(chunks[n + 1])
        gate = conv(u_gate, 0, c0)
        val = conv(u_val, 1, c0)
        act = (gate * jax.nn.sigmoid(gate) * val).astype(BF16)
        res = jnp.dot(act, wdn_ref[c0:c0 + FFN_CHUNK, :], preferred_element_type=F32)
        if last:
            for c in range(nblk):
                os_ref[c] += res[:, c * LANES:(c + 1) * LANES]
        else:
            o_ref[...] += res
    if last:
        for b in range(batch):
            for c in range(nblk):
                o_ref[b, :, c * LANES:(c + 1) * LANES] = os_ref[c, pl.ds(b, steps, stride=batch), :]


def _ffn(x, w, layer, *, batch, mixer=None, last=False):
    rows, d = x.shape
    hidden = w["wdn"].shape[1]
    halo = 2 * batch
    tr = min(FFN_ROW_TILE, rows)
    ts = tr // batch
    row_spec = lambda n: pl.BlockSpec((tr, n), lambda i: (i, 0))
    operands = [x]
    in_specs = [row_spec(d)]
    scratch = [pltpu.VMEM((halo, 2 * hidden), F32)]
    if mixer is not None:
        attn, conv, w_out, mix_layer = mixer
        npair = attn.shape[1]
        operands += [attn, conv, w_out]
        in_specs += [pl.BlockSpec((batch, npair, ts, LANES), lambda i: (0, 0, i, 0)), row_spec(conv.shape[1]),
                     _resident(w_out.shape, mix_layer)]
        scratch.append(pltpu.VMEM((npair, tr, LANES), F32))
    operands += [w["norm"], w["wup"], w["cw"], w["wdn"]]
    in_specs += [_resident(w["norm"].shape), _resident(w["wup"].shape, layer),
                 _resident(w["cw"].shape, layer), _resident(w["wdn"].shape, layer)]
    if last:
        scratch.append(pltpu.VMEM((d // LANES, tr, LANES), F32))
        out_spec = pl.BlockSpec((batch, ts, d), lambda i: (0, i, 0))
        out_shape = jax.ShapeDtypeStruct((batch, rows // batch, d), F32)
    else:
        out_spec = row_spec(d)
        out_shape = jax.ShapeDtypeStruct((rows, d), F32)
    return pl.pallas_call(
        functools.partial(_ffn_kernel, layer=layer, hidden=hidden, halo=halo, batch=batch,
                          mix=mixer is not None, last=last),
        grid=(rows // tr,),
        in_specs=in_specs,
        out_specs=out_spec,
        out_shape=out_shape,
        scratch_shapes=scratch,
        compiler_params=_params("arbitrary"),
        name="conv_ffn",
    )(*operands)


def _ffn_weights(ffn_norm, ffn_w_up, ffn_conv_w, ffn_w_down):
    return dict(norm=ffn_norm, wup=ffn_w_up.astype(BF16), cw=ffn_conv_w, wdn=ffn_w_down.astype(BF16))


def _head_sumsq(x, ones_ref):
    sq = (x * x).astype(BF16)
    parts = [jnp.dot(sq[:, c:c + MXU_TILE], ones_ref[...], preferred_element_type=F32)
             for c in range(0, x.shape[1], MXU_TILE)]
    return jnp.concatenate(parts, axis=1)


def _mixproj_kernel(*refs, layer, halo, batch, first):
    refs = list(refs)
    (x_ref, g_ref, win_ref, cqg_ref, ckvg_ref, wuq_ref, wukv_ref, gains_ref, ones_ref, scw_ref,
     ct_ref, st_ref, q_ref, k_ref, v_ref, cv_ref) = (refs.pop(0) for _ in range(16))
    if first:
        xtm_ref = refs.pop(0)
    carry_ref, qs_ref, ks_ref, vs_ref = (refs.pop(0) for _ in range(4))
    if first:
        xs_ref = refs.pop(0)

    @pl.when(pl.program_id(0) == 0)
    def _():
        carry_ref[...] = jnp.zeros_like(carry_ref)

    if first:
        _to_time_major(x_ref, xs_ref, batch)
        x = jnp.concatenate([xs_ref[c] for c in range(xs_ref.shape[0])], axis=1)
        xtm_ref[...] = x
    else:
        x = x_ref[...]
    rows = x.shape[0]
    steps = rows // batch
    pick = lambda ref: ref[layer:layer + 1, :]
    h = _rms(x, pick(g_ref)).astype(BF16)
    proj = _dot_split(h, win_ref[...])
    o = 0
    c_q = proj[:, o:o + LORA]; o += LORA
    c_kv = proj[:, o:o + LORA]; o += LORA
    k_rope = proj[:, o:o + LANES]; o += LANES
    k_rope_rot = proj[:, o:o + LANES]; o += LANES
    gate_b = proj[:, o:o + CONV_CH]; o += CONV_CH
    gate_c = proj[:, o:o + CONV_CH]; o += CONV_CH
    conv_in = proj[:, o:o + CONV_CH]

    m = gate_c * conv_in
    prev = carry_ref[...]
    carry_ref[...] = m[rows - halo:, :]
    cv_ref[...] = (gate_b * _shift_conv(prev, m, scw_ref[layer])).astype(BF16)

    qq = jnp.dot(_rms(c_q, pick(cqg_ref)).astype(BF16), wuq_ref[...], preferred_element_type=F32)
    kv = jnp.dot(_rms(c_kv, pick(ckvg_ref)).astype(BF16), wukv_ref[...], preferred_element_type=F32)
    q, q_rot = qq[:, :HEAD_LANES], qq[:, HEAD_LANES:]
    kn, v = kv[:, :HEAD_LANES], kv[:, HEAD_LANES:]
    ct, st = ct_ref[...], st_ref[...]
    gains = gains_ref[layer]
    tq, sq = ct * gains[0:1], st * gains[1:2]
    tk, sk = ct * gains[2:3], st * gains[3:4]
    k_nope_gain, v_ones = gains[4:5], gains[5:6]
    inv_dim = 1.0 / QK_DIM
    q_inv = lax.rsqrt(_head_sumsq(q, ones_ref) * inv_dim + EPS)
    kr_sumsq = jnp.sum(k_rope * k_rope, axis=-1, keepdims=True)
    k_inv = lax.rsqrt((_head_sumsq(kn, ones_ref) + kr_sumsq) * inv_dim + EPS)
    kr = k_rope * tk + k_rope_rot * sk
    for hd in range(MLA_HEADS):
        sl = slice(hd * LANES, (hd + 1) * LANES)
        qs_ref[hd] = (q[:, sl] * tq + q_rot[:, sl] * sq) * q_inv[:, sl]
        ks_ref[hd] = (kn[:, sl] * k_nope_gain + kr) * k_inv[:, sl]
        vs_ref[hd] = v[:, sl] + v_ones
    for src, dst in ((qs_ref, q_ref), (ks_ref, k_ref), (vs_ref, v_ref)):
        for b in range(batch):
            for hd in range(MLA_HEADS):
                col = (hd % 2) * LANES
                dst[b, hd // 2, :, col:col + LANES] = src[hd, pl.ds(b, steps, stride=batch), :].astype(BF16)


def _mixproj(x, w, layer, tables, *, batch, first=False):
    if first:
        _, seq, d = x.shape
        rows = seq * batch
    else:
        rows, d = x.shape
    halo = 2 * batch
    tr = min(ROW_TILE, rows)
    ts = tr // batch
    row_spec = lambda n: pl.BlockSpec((tr, n), lambda i: (i, 0))
    pairs = MLA_HEADS // 2
    wide_spec = pl.BlockSpec((batch, pairs, ts, 2 * LANES), lambda i: (0, 0, i, 0))
    wide_shape = jax.ShapeDtypeStruct((batch, pairs, rows // batch, 2 * LANES), BF16)
    x_spec = pl.BlockSpec((batch, ts, d), lambda i: (0, i, 0)) if first else row_spec(d)
    out_specs = [wide_spec, wide_spec, wide_spec, row_spec(CONV_CH)]
    out_shape = [wide_shape, wide_shape, wide_shape, jax.ShapeDtypeStruct((rows, CONV_CH), BF16)]
    scratch = [pltpu.VMEM((halo, CONV_CH), F32)] + [pltpu.VMEM((MLA_HEADS, tr, LANES), F32)] * 3
    if first:
        out_specs.append(row_spec(d))
        out_shape.append(jax.ShapeDtypeStruct((rows, d), F32))
        scratch.append(pltpu.VMEM((d // LANES, tr, LANES), F32))
    return pl.pallas_call(
        functools.partial(_mixproj_kernel, layer=layer, halo=halo, batch=batch, first=first),
        grid=(rows // tr,),
        in_specs=[
            x_spec,
            _resident(w["norm"].shape),
            _resident(w["win"].shape, layer),
            _resident(w["cq_norm"].shape),
            _resident(w["ckv_norm"].shape),
            _resident(w["wuq"].shape, layer),
            _resident(w["wukv"].shape, layer),
            _resident(w["gains"].shape),
            _resident(w["ones_bd"].shape),
            _resident(w["scw"].shape),
            row_spec(LANES), row_spec(LANES),
        ],
        out_specs=out_specs,
        out_shape=out_shape,
        scratch_shapes=scratch,
        compiler_params=_params("arbitrary"),
        name="mix_proj",
    )(x, w["norm"], w["win"], w["cq_norm"], w["ckv_norm"], w["wuq"], w["wukv"], w["gains"],
      w["ones_bd"], w["scw"], *tables)


def _mixproj_weights(attn_norm, mix_w_in, cq_norm, ckv_norm, w_uq, w_ukv, q_gain, k_gain, sconv_w):
    nl, d, _ = mix_w_in.shape
    half = QK_ROPE // 2
    tail = LANES - QK_DIM
    zeros = lambda n: jnp.zeros((nl, d, n), F32)
    o1, o2 = 2 * LORA, 2 * LORA + QK_ROPE
    w_kr = mix_w_in[:, :, o1:o2]
    win = jnp.concatenate(
        [mix_w_in[:, :, :o1],
         zeros(QK_NOPE), w_kr, zeros(tail),
         zeros(QK_NOPE), -w_kr[:, :, half:], w_kr[:, :, :half], zeros(tail),
         mix_w_in[:, :, o2:]], axis=2).astype(BF16)
    wq = w_uq.reshape(nl, LORA, MLA_HEADS, QK_DIM)
    pad = lambda a: jnp.pad(a, ((0, 0), (0, 0), (0, 0), (0, LANES - a.shape[3]))).reshape(nl, LORA, HEAD_LANES)
    wq_rot = jnp.concatenate([jnp.zeros_like(wq[..., :QK_NOPE]), -wq[..., QK_NOPE + half:],
                              wq[..., QK_NOPE:QK_NOPE + half]], axis=3)
    wuq = jnp.concatenate([pad(wq), pad(wq_rot)], axis=2).astype(BF16)
    wkv = w_ukv.reshape(nl, LORA, MLA_HEADS, QK_NOPE + V_DIM)
    wukv = jnp.concatenate([pad(wkv[..., :QK_NOPE]), pad(wkv[..., QK_NOPE:])], axis=2).astype(BF16)

    def rope_gains(g):
        z = jnp.zeros((nl, tail), F32)
        return (jnp.concatenate([g, z], axis=1),
                jnp.concatenate([jnp.zeros((nl, QK_NOPE), F32), g[:, QK_NOPE + half:],
                                 g[:, QK_NOPE:QK_NOPE + half], z], axis=1))

    gq = rope_gains(q_gain * (QK_DIM ** -0.5 * math.log2(math.e)))
    gk = rope_gains(k_gain)
    k_nope_gain = jnp.concatenate([k_gain[:, :QK_NOPE], jnp.zeros((nl, LANES - QK_NOPE), F32)], axis=1)
    v_ones = jnp.broadcast_to((jnp.arange(LANES) == V_DIM).astype(F32), (nl, LANES))
    zero = jnp.zeros((nl, LANES), F32)
    gains = jnp.stack([*gq, *gk, k_nope_gain, v_ones, zero, zero], axis=1)
    lane_head = jnp.arange(MXU_TILE) // LANES
    ones_bd = (lane_head[:, None] == lane_head[None, :]).astype(BF16)
    return dict(norm=attn_norm, win=win, cq_norm=cq_norm, ckv_norm=ckv_norm, wuq=wuq, wukv=wukv,
                gains=gains, ones_bd=ones_bd, scw=sconv_w)


def _rope_tables(seq, batch):
    inv_freq = 1.0 / (ROPE_THETA ** (jnp.arange(0, QK_ROPE, 2, dtype=F32) / QK_ROPE))
    ang = jnp.arange(seq, dtype=F32)[:, None] * inv_freq[None, :]
    cos, sin = jnp.cos(ang), jnp.sin(ang)
    z = lambda n: jnp.zeros((seq, n), F32)
    tail = LANES - QK_DIM
    ct = jnp.concatenate([jnp.ones((seq, QK_NOPE), F32), cos, cos, z(tail)], axis=1)
    st = jnp.concatenate([z(QK_NOPE), sin, sin, z(tail)], axis=1)
    return tuple(jnp.repeat(t, batch, axis=0) for t in (ct, st))


def _attn_kernel(q_ref, k_ref, v_ref, o_ref, sa_ref, sb_ref, m_ref, acc_ref, *, tq):
    tk = tq // 2
    qi = pl.program_id(2)
    heads = [slice(hh * LANES, (hh + 1) * LANES) for hh in range(2)]
    nt = (((1,), (1,)), ((), ()))

    def scores(kb, dst, r0=0):
        start = pl.multiple_of(kb * tk, tk)
        for hh, sl in enumerate(heads):
            dst[hh, 0:tq - r0, :] = lax.dot_general(q_ref[r0:, sl], k_ref[pl.ds(start, tk), sl], nt,
                                                    preferred_element_type=F32)

    def update(kb, src, r0=0, diagonal=False):
        start = pl.multiple_of(kb * tk, tk)
        n = tq - r0
        for hh, sl in enumerate(heads):
            s = src[hh, 0:n, :]
            if diagonal:
                row = lax.broadcasted_iota(jnp.int32, (n, tk), 0)
                col = lax.broadcasted_iota(jnp.int32, (n, tk), 1)
                s = jnp.where(col <= row, s, -jnp.inf)
            m_old = m_ref[hh, r0:, :]
            m_new = jnp.maximum(m_old, jnp.max(s, axis=-1, keepdims=True))
            p = jnp.concatenate(
                [jnp.exp2((s[:, c:c + LANES] - m_new).astype(BF16)) for c in range(0, tk, LANES)], axis=1)
            pv = jnp.dot(p, v_ref[pl.ds(start, tk), sl], preferred_element_type=F32)
            acc_ref[hh, r0:, :] = jnp.exp2(m_old - m_new) * acc_ref[hh, r0:, :] + pv
            m_ref[hh, r0:, :] = m_new

    m_ref[...] = jnp.full(m_ref.shape, -jnp.inf, F32)
    acc_ref[...] = jnp.zeros(acc_ref.shape, F32)
    scores(0, sa_ref)

    def body(j, _):
        scores(2 * j + 1, sb_ref)
        update(2 * j, sa_ref)
        scores(2 * j + 2, sa_ref)
        update(2 * j + 1, sb_ref)
        return 0

    lax.fori_loop(0, qi, body, 0)
    scores(2 * qi + 1, sb_ref, r0=tk)
    update(2 * qi, sa_ref, diagonal=True)
    update(2 * qi + 1, sb_ref, r0=tk, diagonal=True)
    lane = lax.broadcasted_iota(jnp.int32, (1, LANES), 1)
    outs = []
    for hh in range(2):
        acc = acc_ref[hh]
        outs.append(acc / acc[:, V_DIM:V_DIM + 1])
    o_ref[...] = jnp.where(lane < V_DIM, outs[0], pltpu.roll(outs[1], V_DIM, 1))


def _attention(q, k, v, *, seq, batch):
    tq = min(ATTN_TILE, seq)
    pairs = MLA_HEADS // 2
    return pl.pallas_call(
        functools.partial(_attn_kernel, tq=tq),
        grid=(batch, pairs, seq // tq),
        in_specs=[
            pl.BlockSpec((None, None, tq, 2 * LANES), lambda b, j, i: (b, j, i, 0)),
            pl.BlockSpec((None, None, seq, 2 * LANES), lambda b, j, i: (b, j, 0, 0)),
            pl.BlockSpec((None, None, seq, 2 * LANES), lambda b, j, i: (b, j, 0, 0)),
        ],
        out_specs=pl.BlockSpec((None, None, tq, LANES), lambda b, j, i: (b, j, i, 0)),
        out_shape=jax.ShapeDtypeStruct((batch, pairs, seq, LANES), F32),
        scratch_shapes=[
            pltpu.VMEM((2, tq, tq // 2), F32),
            pltpu.VMEM((2, tq, tq // 2), F32),
            pltpu.VMEM((2, tq, LANES), F32),
            pltpu.VMEM((2, tq, LANES), F32),
        ],
        compiler_params=_params("arbitrary", "arbitrary", "arbitrary"),
        name="causal_attention",
    )(q, k, v)


def _s5_discretize_kernel(lr_ref, li_ref, dt_ref, lre_ref, lie_ref, dte_ref, bre_ref, bim_ref,
                          ar_ref, ai_ref, bbr_ref, bbi_ref):
    def zoh(lr, li, dt):
        mag = jnp.exp(lr * dt)
        ar, ai = mag * jnp.cos(li * dt), mag * jnp.sin(li * dt)
        nr, ni = ar - 1.0, ai
        den = lr * lr + li * li
        return ar, ai, (nr * lr + ni * li) / den, (ni * lr - nr * li) / den

    ar, ai, _, _ = zoh(lr_ref[...], li_ref[...], dt_ref[...])
    ar_ref[...] = ar
    ai_ref[...] = ai
    _, _, zr, zi = zoh(lre_ref[...], lie_ref[...], dte_ref[...])
    br, bi = bre_ref[...], bim_ref[...]
    bbr_ref[...] = zr * br - zi * bi
    bbi_ref[...] = zr * bi + zi * br


def _s5_discretize(lambda_re, lambda_im, log_step, b_re, b_im):
    g, p = lambda_re.shape
    c = b_re.shape[-1]
    dt = jnp.broadcast_to(jnp.exp(log_step)[:, None], (g, p))
    expand = lambda a: jnp.repeat(a, c, axis=1)
    flat = lambda a: a.reshape(g, p * c)
    out = pl.pallas_call(
        _s5_discretize_kernel,
        out_shape=[jax.ShapeDtypeStruct((g, p), F32)] * 2 + [jax.ShapeDtypeStruct((g, p * c), F32)] * 2,
        name="s5_discretize",
    )(lambda_re, lambda_im, dt, expand(lambda_re), expand(lambda_im), expand(dt), flat(b_re), flat(b_im))
    ar, ai, bbr, bbi = out
    return ar, ai, bbr.reshape(g, p, c), bbi.reshape(g, p, c)


def _s5_kernel(x_ref, g_ref, win_ref, bm_ref, a_ref, cm_ref, d_ref, wglu_ref, o_ref,
               state_ref, u_ref, bu_ref, st_ref, y_ref, *, layer, batch, nblock, half):
    @pl.when(pl.program_id(0) == 0)
    def _():
        state_ref[...] = jnp.zeros_like(state_ref)

    x = x_ref[...]
    rows, d = x.shape
    steps = rows // batch
    h = _rms(x, g_ref[layer:layer + 1, :]).astype(BF16)
    u = _dot_split(h, win_ref[...])
    u_ref[...] = u
    ub = u.astype(BF16)
    lanes = lambda blk: slice(blk * LANES, (blk + 1) * LANES)

    def project_in(blk):
        bu_ref[blk % 2] = _dot_split(ub[:, lanes(blk)], bm_ref[blk])

    def project_out(blk):
        y_ref[:, lanes(blk)] = _dot_split(st_ref[blk % 2].astype(BF16), cm_ref[blk])

    project_in(0)
    for blk in range(nblock):
        if blk + 1 < nblock:
            project_in(blk + 1)
        if blk >= 1:
            project_out(blk - 1)
        slot = blk % 2
        ar = a_ref[blk, 0:batch, :]
        ai = a_ref[blk, batch:2 * batch, :]
        sr = state_ref[blk, 0:batch, :]
        si = state_ref[blk, batch:2 * batch, :]
        for t in range(steps):
            r = slice(t * batch, (t + 1) * batch)
            sr, si = (ar * sr - ai * si + bu_ref[slot, r, 0:half],
                      ar * si + ai * sr + bu_ref[slot, r, half:2 * half])
            st_ref[slot, r, 0:half] = sr
            st_ref[slot, r, half:2 * half] = si
        state_ref[blk, 0:batch, :] = sr
        state_ref[blk, batch:2 * batch, :] = si
    project_out(nblock - 1)
    y = y_ref[...] + d_ref[layer:layer + 1, :] * u_ref[...]
    act = jax.nn.gelu(y).astype(BF16)
    z = _dot_split(act, wglu_ref[...])
    o_ref[...] = x + z[:, :d] * jax.nn.sigmoid(z[:, d:])


def _s5(x, w, layer, *, batch):
    rows, d = x.shape
    _, nblock, _, width = w["bm"].shape
    half = width // 2
    tr = min(ROW_TILE, rows)
    return pl.pallas_call(
        functools.partial(_s5_kernel, layer=layer, batch=batch, nblock=nblock, half=half),
        grid=(rows // tr,),
        in_specs=[
            pl.BlockSpec((tr, d), lambda i: (i, 0)),
            _resident(w["norm"].shape),
            _resident(w["win"].shape, layer),
            _resident(w["bm"].shape, layer),
            _resident(w["a"].shape, layer),
            _resident(w["cm"].shape, layer),
            _resident(w["d_skip"].shape),
            _resident(w["wglu"].shape, layer),
        ],
        out_specs=pl.BlockSpec((tr, d), lambda i: (i, 0)),
        out_shape=jax.ShapeDtypeStruct((rows, d), F32),
        scratch_shapes=[
            pltpu.VMEM((nblock, 2 * batch, half), F32),
            pltpu.VMEM((tr, d), F32),
            pltpu.VMEM((2, tr, 2 * half), F32),
            pltpu.VMEM((2, tr, 2 * half), F32),
            pltpu.VMEM((tr, d), F32),
        ],
        compiler_params=_params("arbitrary"),
        name="s5_mixer",
    )(x, w["norm"], w["win"], w["bm"], w["a"], w["cm"], w["d_skip"], w["wglu"])


def _s5_weights(ssm_norm, ssm_w_in, lambda_re, lambda_im, log_step, b_re, b_im, c_re, c_im, d_skip, w_glu,
                *, batch):
    nl, groups, nstate = lambda_re.shape
    gpb = GROUPS_PER_BLOCK
    nblock = groups // gpb
    half = gpb * nstate
    merge = lambda a: a.reshape((nl * groups,) + a.shape[2:])
    ar, ai, bbr, bbi = _s5_discretize(merge(lambda_re), merge(lambda_im), merge(log_step),
                                      merge(b_re), merge(b_im))
    eye = jnp.eye(gpb, dtype=F32)

    def in_mat(bb):
        bb = bb.reshape(nl, nblock, gpb, nstate, SSM_GROUP)
        return jnp.einsum("LGgpc,hg->LGhcgp", bb, eye).reshape(nl, nblock, LANES, half)

    def out_mat(cc):
        cc = cc.reshape(nl, nblock, gpb, SSM_GROUP, nstate)
        return jnp.einsum("LGgcp,gh->LGgphc", cc, eye).reshape(nl, nblock, half, LANES)

    bm = jnp.concatenate([in_mat(bbr), in_mat(bbi)], axis=3).astype(BF16)
    cm = jnp.concatenate([out_mat(c_re), out_mat(-c_im)], axis=2).astype(BF16)
    rep = lambda a: jnp.broadcast_to(a.reshape(nl, nblock, 1, half), (nl, nblock, batch, half))
    a_mat = jnp.concatenate([rep(ar), rep(ai)], axis=2)
    return dict(norm=ssm_norm, win=ssm_w_in.astype(BF16), bm=bm, a=a_mat, cm=cm, d_skip=d_skip,
                wglu=w_glu.astype(BF16))


def kernel(x, attn_norm, mix_w_in, cq_norm, ckv_norm, w_uq, w_ukv, q_gain, k_gain, sconv_w, mix_w_out,
           ssm_norm, ssm_w_in, lambda_re, lambda_im, log_step, b_re, b_im, c_re, c_im, d_skip, w_glu,
           ffn_norm, ffn_w_up, ffn_conv_w, ffn_w_down):
    batch, seq, d = x.shape
    depth = ffn_norm.shape[0]
    tables = _rope_tables(seq, batch)
    mix_w = _mixproj_weights(attn_norm, mix_w_in, cq_norm, ckv_norm, w_uq, w_ukv, q_gain, k_gain, sconv_w)
    w_out = mix_w_out.astype(BF16)
    s5_w = _s5_weights(ssm_norm, ssm_w_in, lambda_re, lambda_im, log_step, b_re, b_im, c_re, c_im,
                       d_skip, w_glu, batch=batch)
    ffn_w = _ffn_weights(ffn_norm, ffn_w_up, ffn_conv_w, ffn_w_down)
    xt = x
    for layer in range(depth):
        i = layer // 2
        mixer = None
        if layer % 2 == 0:
            first = layer == 0
            outs = _mixproj(xt, mix_w, i, tables, batch=batch, first=first)
            q, k, v, conv = outs[:4]
            if first:
                xt = outs[4]
            mixer = (_attention(q, k, v, seq=seq, batch=batch), conv, w_out, i)
        else:
            xt = _s5(xt, s5_w, i, batch=batch)
        xt = _ffn(xt, ffn_w, layer, batch=batch, mixer=mixer, last=layer == depth - 1)
    return xt
```

```python
import functools
import math

import jax
import jax.numpy as jnp
from jax import lax
from jax.experimental import pallas as pl
from jax.experimental.pallas import tpu as pltpu

F32 = jnp.float32
BF16 = jnp.bfloat16

EPS = 1e-6
ROPE_THETA = 10000.0
LANES = 128
MXU_TILE = 256
MLA_HEADS = 8
QK_NOPE = 64
QK_ROPE = 32
QK_DIM = QK_NOPE + QK_ROPE
V_DIM = 64
LORA = 256
CONV_CH = 512
SSM_GROUP = 16
GROUPS_PER_BLOCK = LANES // SSM_GROUP
HEAD_LANES = MLA_HEADS * LANES
VMEM_LIMIT = 56 * 1024 * 1024
ROW_TILE = 1024
MIX_ROW_TILE = 512
ATTN_TILE = 1024
FFN_CHUNK = MXU_TILE


def _params(*sem):
    return pltpu.CompilerParams(dimension_semantics=sem, vmem_limit_bytes=VMEM_LIMIT)


def _resident(shape, layer=None):
    if layer is None:
        nd = len(shape)
        return pl.BlockSpec(shape, lambda *_: (0,) * nd, pipeline_mode=pl.Buffered(1))
    nd = len(shape) - 1
    return pl.BlockSpec((None,) + tuple(shape[1:]), lambda *_: (layer,) + (0,) * nd,
                        pipeline_mode=pl.Buffered(1))


def _rms(x, gain):
    return x * lax.rsqrt(jnp.mean(x * x, axis=-1, keepdims=True) + EPS) * gain


def _dot_split(a, w, parts=2):
    rows = a.shape[0]
    step = rows // parts
    return jnp.concatenate(
        [jnp.dot(a[r:r + step], w, preferred_element_type=F32) for r in range(0, rows, step)], axis=0)


def _shift_conv(prev, cur, w):
    rows = cur.shape[0]
    step = prev.shape[0] // 2
    ext = jnp.concatenate([prev, cur], axis=0)
    return w[0:1] * ext[0:rows] + w[1:2] * ext[step:rows + step] + w[2:3] * cur


def _to_time_major(src_ref, dst_ref, batch):
    steps = src_ref.shape[1]
    for b in range(batch):
        for c in range(dst_ref.shape[0]):
            dst_ref[c, pl.ds(b, steps, stride=batch), :] = src_ref[b, :, c * LANES:(c + 1) * LANES]


def _ffn_kernel(*refs, layer, hidden, halo, batch, mix, last):
    refs = list(refs)
    x_ref = refs.pop(0)
    if mix:
        a_ref, cv_ref, wo_ref = refs.pop(0), refs.pop(0), refs.pop(0)
    g_ref, wup_ref, cw_ref, wdn_ref, o_ref, carry_ref = (refs.pop(0) for _ in range(6))
    if mix:
        ail_ref = refs.pop(0)
    if last:
        os_ref = refs.pop(0)

    @pl.when(pl.program_id(0) == 0)
    def _():
        carry_ref[...] = jnp.zeros_like(carry_ref)

    x = x_ref[...]
    rows, d = x.shape
    steps = rows // batch
    nblk = d // LANES
    if mix:
        npair = a_ref.shape[1]
        na = npair * LANES
        for b in range(batch):
            for c in range(npair):
                ail_ref[c, pl.ds(b, steps, stride=batch), :] = a_ref[b, c]
        attn = jnp.concatenate([ail_ref[c] for c in range(npair)], axis=1).astype(BF16)
        x = (x + jnp.dot(attn, wo_ref[:na, :], preferred_element_type=F32)
             + jnp.dot(cv_ref[...], wo_ref[na:, :], preferred_element_type=F32))
    h = _rms(x, g_ref[layer:layer + 1, :]).astype(BF16)
    if last:
        for c in range(nblk):
            os_ref[c] = x[:, c * LANES:(c + 1) * LANES]
    else:
        o_ref[...] = x

    def up_project(c0):
        return tuple(_dot_split(h, wup_ref[:, j * hidden + c0:j * hidden + c0 + FFN_CHUNK]) for j in range(2))

    def conv(u, j, c0):
        cols = slice(j * hidden + c0, j * hidden + c0 + FFN_CHUNK)
        prev = carry_ref[:, cols]
        carry_ref[:, cols] = u[rows - halo:, :]
        return _shift_conv(prev, u, cw_ref[:, cols])

    chunks = list(range(0, hidden, FFN_CHUNK))
    u_next = up_project(chunks[0])
    for n, c0 in enumerate(chunks):
        u_gate, u_val = u_next
        if n + 1 < len(chunks):
            u_next = up_project(chunks[n + 1])
        gate = conv(u_gate, 0, c0)
        val = conv(u_val, 1, c0)
        act = (gate * jax.nn.sigmoid(gate) * val).astype(BF16)
        res = jnp.dot(act, wdn_ref[c0:c0 + FFN_CHUNK, :], preferred_element_type=F32)
        if last:
            for c in range(nblk):
                os_ref[c] += res[:, c * LANES:(c + 1) * LANES]
        else:
            o_ref[...] += res
    if last:
        for b in range(batch):
            for c in range(nblk):
                o_ref[b, :, c * LANES:(c + 1) * LANES] = os_ref[c, pl.ds(b, steps, stride=batch), :]


def _ffn(x, w, layer, *, batch, mixer=None, last=False):
    rows, d = x.shape
    hidden = w["wdn"].shape[1]
    halo = 2 * batch
    tr = min(ROW_TILE, rows)
    ts = tr // batch
    row_spec = lambda n: pl.BlockSpec((tr, n), lambda i: (i, 0))
    operands = [x]
    in_specs = [row_spec(d)]
    scratch = [pltpu.VMEM((halo, 2 * hidden), F32)]
    if mixer is not None:
        attn, conv, w_out, mix_layer = mixer
        npair = attn.shape[1]
        operands += [attn, conv, w_out]
        in_specs += [pl.BlockSpec((batch, npair, ts, LANES), lambda i: (0, 0, i, 0)), row_spec(conv.shape[1]),
                     _resident(w_out.shape, mix_layer)]
        scratch.append(pltpu.VMEM((npair, tr, LANES), F32))
    operands += [w["norm"], w["wup"], w["cw"], w["wdn"]]
    in_specs += [_resident(w["norm"].shape), _resident(w["wup"].shape, layer),
                 _resident(w["cw"].shape, layer), _resident(w["wdn"].shape, layer)]
    if last:
        scratch.append(pltpu.VMEM((d // LANES, tr, LANES), F32))
        out_spec = pl.BlockSpec((batch, ts, d), lambda i: (0, i, 0))
        out_shape = jax.ShapeDtypeStruct((batch, rows // batch, d), F32)
    else:
        out_spec = row_spec(d)
        out_shape = jax.ShapeDtypeStruct((rows, d), F32)
    return pl.pallas_call(
        functools.partial(_ffn_kernel, layer=layer, hidden=hidden, halo=halo, batch=batch,
                          mix=mixer is not None, last=last),
        grid=(rows // tr,),
        in_specs=in_specs,
        out_specs=out_spec,
        out_shape=out_shape,
        scratch_shapes=scratch,
        compiler_params=_params("arbitrary"),
        name="conv_ffn",
    )(*operands)


def _ffn_weights(ffn_norm, ffn_w_up, ffn_conv_w, ffn_w_down):
    return dict(norm=ffn_norm, wup=ffn_w_up.astype(BF16), cw=ffn_conv_w, wdn=ffn_w_down.astype(BF16))


def _head_sumsq(x, ones_ref):
    sq = (x * x).astype(BF16)
    parts = [jnp.dot(sq[:, c:c + MXU_TILE], ones_ref[...], preferred_element_type=F32)
             for c in range(0, x.shape[1], MXU_TILE)]
    return jnp.concatenate(parts, axis=1)


def _mixproj_kernel(*refs, layer, halo, batch, first):
    refs = list(refs)
    (x_ref, g_ref, win_ref, cqg_ref, ckvg_ref, wuq_ref, wukv_ref, gains_ref, ones_ref, scw_ref,
     ct_ref, st_ref, q_ref, k_ref, v_ref, cv_ref) = (refs.pop(0) for _ in range(16))
    if first:
        xtm_ref = refs.pop(0)
    carry_ref, qs_ref, ks_ref, vs_ref = (refs.pop(0) for _ in range(4))
    if first:
        xs_ref = refs.pop(0)

    @pl.when(pl.program_id(0) == 0)
    def _():
        carry_ref[...] = jnp.zeros_like(carry_ref)

    if first:
        _to_time_major(x_ref, xs_ref, batch)
        x = jnp.concatenate([xs_ref[c] for c in range(xs_ref.shape[0])], axis=1)
        xtm_ref[...] = x
    else:
        x = x_ref[...]
    rows = x.shape[0]
    steps = rows // batch
    pick = lambda ref: ref[layer:layer + 1, :]
    h = _rms(x, pick(g_ref)).astype(BF16)
    proj = _dot_split(h, win_ref[...])
    o = 0
    c_q = proj[:, o:o + LORA]; o += LORA
    c_kv = proj[:, o:o + LORA]; o += LORA
    k_rope = proj[:, o:o + LANES]; o += LANES
    k_rope_rot = proj[:, o:o + LANES]; o += LANES
    gate_b = proj[:, o:o + CONV_CH]; o += CONV_CH
    gate_c = proj[:, o:o + CONV_CH]; o += CONV_CH
    conv_in = proj[:, o:o + CONV_CH]

    m = gate_c * conv_in
    prev = carry_ref[...]
    carry_ref[...] = m[rows - halo:, :]
    cv_ref[...] = (gate_b * _shift_conv(prev, m, scw_ref[layer])).astype(BF16)

    qq = jnp.dot(_rms(c_q, pick(cqg_ref)).astype(BF16), wuq_ref[...], preferred_element_type=F32)
    kv = jnp.dot(_rms(c_kv, pick(ckvg_ref)).astype(BF16), wukv_ref[...], preferred_element_type=F32)
    q, q_rot = qq[:, :HEAD_LANES], qq[:, HEAD_LANES:]
    kn, v = kv[:, :HEAD_LANES], kv[:, HEAD_LANES:]
    ct, st = ct_ref[...], st_ref[...]
    gains = gains_ref[layer]
    tq, sq = ct * gains[0:1], st * gains[1:2]
    tk, sk = ct * gains[2:3], st * gains[3:4]
    k_nope_gain, v_ones = gains[4:5], gains[5:6]
    inv_dim = 1.0 / QK_DIM
    q_inv = lax.rsqrt(_head_sumsq(q, ones_ref) * inv_dim + EPS)
    kr_sumsq = jnp.sum(k_rope * k_rope, axis=-1, keepdims=True)
    k_inv = lax.rsqrt((_head_sumsq(kn, ones_ref) + kr_sumsq) * inv_dim + EPS)
    kr = k_rope * tk + k_rope_rot * sk
    for hd in range(MLA_HEADS):
        sl = slice(hd * LANES, (hd + 1) * LANES)
        qs_ref[hd] = (q[:, sl] * tq + q_rot[:, sl] * sq) * q_inv[:, sl]
        ks_ref[hd] = (kn[:, sl] * k_nope_gain + kr) * k_inv[:, sl]
        vs_ref[hd] = v[:, sl] + v_ones
    for src, dst in ((qs_ref, q_ref), (ks_ref, k_ref), (vs_ref, v_ref)):
        for b in range(batch):
            for hd in range(MLA_HEADS):
                col = (hd % 2) * LANES
                dst[b, hd // 2, :, col:col + LANES] = src[hd, pl.ds(b, steps, stride=batch), :].astype(BF16)


def _mixproj(x, w, layer, tables, *, batch, first=False):
    if first:
        _, seq, d = x.shape
        rows = seq * batch
    else:
        rows, d = x.shape
    halo = 2 * batch
    tr = min(MIX_ROW_TILE, rows)
    ts = tr // batch
    row_spec = lambda n: pl.BlockSpec((tr, n), lambda i: (i, 0))
    pairs = MLA_HEADS // 2
    wide_spec = pl.BlockSpec((batch, pairs, ts, 2 * LANES), lambda i: (0, 0, i, 0))
    wide_shape = jax.ShapeDtypeStruct((batch, pairs, rows // batch, 2 * LANES), BF16)
    x_spec = pl.BlockSpec((batch, ts, d), lambda i: (0, i, 0)) if first else row_spec(d)
    out_specs = [wide_spec, wide_spec, wide_spec, row_spec(CONV_CH)]
    out_shape = [wide_shape, wide_shape, wide_shape, jax.ShapeDtypeStruct((rows, CONV_CH), BF16)]
    scratch = [pltpu.VMEM((halo, CONV_CH), F32)] + [pltpu.VMEM((MLA_HEADS, tr, LANES), F32)] * 3
    if first:
        out_specs.append(row_spec(d))
        out_shape.append(jax.ShapeDtypeStruct((rows, d), F32))
        scratch.append(pltpu.VMEM((d // LANES, tr, LANES), F32))
    return pl.pallas_call(
        functools.partial(_mixproj_kernel, layer=layer, halo=halo, batch=batch, first=first),
        grid=(rows // tr,),
        in_specs=[
            x_spec,
            _resident(w["norm"].shape),
            _resident(w["win"].shape, layer),
            _resident(w["cq_norm"].shape),
            _resident(w["ckv_norm"].shape),
            _resident(w["wuq"].shape, layer),
            _resident(w["wukv"].shape, layer),
            _resident(w["gains"].shape),
            _resident(w["ones_bd"].shape),
            _resident(w["scw"].shape),
            row_spec(LANES), row_spec(LANES),
        ],
        out_specs=out_specs,
        out_shape=out_shape,
        scratch_shapes=scratch,
        compiler_params=_params("arbitrary"),
        name="mix_proj",
    )(x, w["norm"], w["win"], w["cq_norm"], w["ckv_norm"], w["wuq"], w["wukv"], w["gains"],
      w["ones_bd"], w["scw"], *tables)


def _mixproj_weights(attn_norm, mix_w_in, cq_norm, ckv_norm, w_uq, w_ukv, q_gain, k_gain, sconv_w):
    nl, d, _ = mix_w_in.shape
    half = QK_ROPE // 2
    tail = LANES - QK_DIM
    zeros = lambda n: jnp.zeros((nl, d, n), F32)
    o1, o2 = 2 * LORA, 2 * LORA + QK_ROPE
    w_kr = mix_w_in[:, :, o1:o2]
    win = jnp.concatenate(
        [mix_w_in[:, :, :o1],
         zeros(QK_NOPE), w_kr, zeros(tail),
         zeros(QK_NOPE), -w_kr[:, :, half:], w_kr[:, :, :half], zeros(tail),
         mix_w_in[:, :, o2:]], axis=2).astype(BF16)
    wq = w_uq.reshape(nl, LORA, MLA_HEADS, QK_DIM)
    pad = lambda a: jnp.pad(a, ((0, 0), (0, 0), (0, 0), (0, LANES - a.shape[3]))).reshape(nl, LORA, HEAD_LANES)
    wq_rot = jnp.concatenate([jnp.zeros_like(wq[..., :QK_NOPE]), -wq[..., QK_NOPE + half:],
                              wq[..., QK_NOPE:QK_NOPE + half]], axis=3)
    wuq = jnp.concatenate([pad(wq), pad(wq_rot)], axis=2).astype(BF16)
    wkv = w_ukv.reshape(nl, LORA, MLA_HEADS, QK_NOPE + V_DIM)
    wukv = jnp.concatenate([pad(wkv[..., :QK_NOPE]), pad(wkv[..., QK_NOPE:])], axis=2).astype(BF16)

    def rope_gains(g):
        z = jnp.zeros((nl, tail), F32)
        return (jnp.concatenate([g, z], axis=1),
                jnp.concatenate([jnp.zeros((nl, QK_NOPE), F32), g[:, QK_NOPE + half:],
                                 g[:, QK_NOPE:QK_NOPE + half], z], axis=1))

    gq = rope_gains(q_gain * (QK_DIM ** -0.5 * math.log2(math.e)))
    gk = rope_gains(k_gain)
    k_nope_gain = jnp.concatenate([k_gain[:, :QK_NOPE], jnp.zeros((nl, LANES - QK_NOPE), F32)], axis=1)
    v_ones = jnp.broadcast_to((jnp.arange(LANES) == V_DIM).astype(F32), (nl, LANES))
    zero = jnp.zeros((nl, LANES), F32)
    gains = jnp.stack([*gq, *gk, k_nope_gain, v_ones, zero, zero], axis=1)
    lane_head = jnp.arange(MXU_TILE) // LANES
    ones_bd = (lane_head[:, None] == lane_head[None, :]).astype(BF16)
    return dict(norm=attn_norm, win=win, cq_norm=cq_norm, ckv_norm=ckv_norm, wuq=wuq, wukv=wukv,
                gains=gains, ones_bd=ones_bd, scw=sconv_w)


def _rope_tables(seq, batch):
    inv_freq = 1.0 / (ROPE_THETA ** (jnp.arange(0, QK_ROPE, 2, dtype=F32) / QK_ROPE))
    ang = jnp.arange(seq, dtype=F32)[:, None] * inv_freq[None, :]
    cos, sin = jnp.cos(ang), jnp.sin(ang)
    z = lambda n: jnp.zeros((seq, n), F32)
    tail = LANES - QK_DIM
    ct = jnp.concatenate([jnp.ones((seq, QK_NOPE), F32), cos, cos, z(tail)], axis=1)
    st = jnp.concatenate([z(QK_NOPE), sin, sin, z(tail)], axis=1)
    return tuple(jnp.repeat(t, batch, axis=0) for t in (ct, st))


def _attn_kernel(q_ref, k_ref, v_ref, o_ref, sa_ref, sb_ref, m_ref, acc_ref, *, tq):
    tk = tq // 2
    qi = pl.program_id(2)
    heads = [slice(hh * LANES, (hh + 1) * LANES) for hh in range(2)]
    nt = (((1,), (1,)), ((), ()))

    def scores(kb, dst, r0=0):
        start = pl.multiple_of(kb * tk, tk)
        for hh, sl in enumerate(heads):
            dst[hh, 0:tq - r0, :] = lax.dot_general(q_ref[r0:, sl], k_ref[pl.ds(start, tk), sl], nt,
                                                    preferred_element_type=F32)

    def update(kb, src, r0=0, diagonal=False):
        start = pl.multiple_of(kb * tk, tk)
        n = tq - r0
        for hh, sl in enumerate(heads):
            s = src[hh, 0:n, :]
            if diagonal:
                row = lax.broadcasted_iota(jnp.int32, (n, tk), 0)
                col = lax.broadcasted_iota(jnp.int32, (n, tk), 1)
                s = jnp.where(col <= row, s, -jnp.inf)
            m_old = m_ref[hh, r0:, :]
            m_new = jnp.maximum(m_old, jnp.max(s, axis=-1, keepdims=True))
            p = jnp.concatenate(
                [jnp.exp2((s[:, c:c + LANES] - m_new).astype(BF16)) for c in range(0, tk, LANES)], axis=1)
            pv = jnp.dot(p, v_ref[pl.ds(start, tk), sl], preferred_element_type=F32)
            acc_ref[hh, r0:, :] = jnp.exp2(m_old - m_new) * acc_ref[hh, r0:, :] + pv
            m_ref[hh, r0:, :] = m_new

    m_ref[...] = jnp.full(m_ref.shape, -jnp.inf, F32)
    acc_ref[...] = jnp.zeros(acc_ref.shape, F32)
    scores(0, sa_ref)

    def body(j, _):
        scores(2 * j + 1, sb_ref)
        update(2 * j, sa_ref)
        scores(2 * j + 2, sa_ref)
        update(2 * j + 1, sb_ref)
        return 0

    lax.fori_loop(0, qi, body, 0)
    scores(2 * qi + 1, sb_ref, r0=tk)
    update(2 * qi, sa_ref, diagonal=True)
    update(2 * qi + 1, sb_ref, r0=tk, diagonal=True)
    lane = lax.broadcasted_iota(jnp.int32, (1, LANES), 1)
    outs = []
    for hh in range(2):
        acc = acc_ref[hh]
        outs.append(acc / acc[:, V_DIM:V_DIM + 1])
    o_ref[...] = jnp.where(lane < V_DIM, outs[0], pltpu.roll(outs[1], V_DIM, 1))


def _attention(q, k, v, *, seq, batch):
    tq = min(ATTN_TILE, seq)
    pairs = MLA_HEADS // 2
    return pl.pallas_call(
        functools.partial(_attn_kernel, tq=tq),
        grid=(batch, pairs, seq // tq),
        in_specs=[
            pl.BlockSpec((None, None, tq, 2 * LANES), lambda b, j, i: (b, j, i, 0)),
            pl.BlockSpec((None, None, seq, 2 * LANES), lambda b, j, i: (b, j, 0, 0)),
            pl.BlockSpec((None, None, seq, 2 * LANES), lambda b, j, i: (b, j, 0, 0)),
        ],
        out_specs=pl.BlockSpec((None, None, tq, LANES), lambda b, j, i: (b, j, i, 0)),
        out_shape=jax.ShapeDtypeStruct((batch, pairs, seq, LANES), F32),
        scratch_shapes=[
            pltpu.VMEM((2, tq, tq // 2), F32),
            pltpu.VMEM((2, tq, tq // 2), F32),
            pltpu.VMEM((2, tq, LANES), F32),
            pltpu.VMEM((2, tq, LANES), F32),
        ],
        compiler_params=_params("arbitrary", "arbitrary", "arbitrary"),
        name="causal_attention",
    )(q, k, v)


def _s5_discretize_kernel(lr_ref, li_ref, dt_ref, lre_ref, lie_ref, dte_ref, bre_ref, bim_ref,
                          ar_ref, ai_ref, bbr_ref, bbi_ref):
    def zoh(lr, li, dt):
        mag = jnp.exp(lr * dt)
        ar, ai = mag * jnp.cos(li * dt), mag * jnp.sin(li * dt)
        nr, ni = ar - 1.0, ai
        den = lr * lr + li * li
        return ar, ai, (nr * lr + ni * li) / den, (ni * lr - nr * li) / den

    ar, ai, _, _ = zoh(lr_ref[...], li_ref[...], dt_ref[...])
    ar_ref[...] = ar
    ai_ref[...] = ai
    _, _, zr, zi = zoh(lre_ref[...], lie_ref[...], dte_ref[...])
    br, bi = bre_ref[...], bim_ref[...]
    bbr_ref[...] = zr * br - zi * bi
    bbi_ref[...] = zr * bi + zi * br


def _s5_discretize(lambda_re, lambda_im, log_step, b_re, b_im):
    g, p = lambda_re.shape
    c = b_re.shape[-1]
    dt = jnp.broadcast_to(jnp.exp(log_step)[:, None], (g, p))
    expand = lambda a: jnp.repeat(a, c, axis=1)
    flat = lambda a: a.reshape(g, p * c)
    out = pl.pallas_call(
        _s5_discretize_kernel,
        out_shape=[jax.ShapeDtypeStruct((g, p), F32)] * 2 + [jax.ShapeDtypeStruct((g, p * c), F32)] * 2,
        name="s5_discretize",
    )(lambda_re, lambda_im, dt, expand(lambda_re), expand(lambda_im), expand(dt), flat(b_re), flat(b_im))
    ar, ai, bbr, bbi = out
    return ar, ai, bbr.reshape(g, p, c), bbi.reshape(g, p, c)


def _s5_kernel(x_ref, g_ref, win_ref, bm_ref, a_ref, cm_ref, d_ref, wglu_ref, o_ref,
               state_ref, u_ref, bu_ref, st_ref, y_ref, *, layer, batch, nblock, half):
    @pl.when(pl.program_id(0) == 0)
    def _():
        state_ref[...] = jnp.zeros_like(state_ref)

    x = x_ref[...]
    rows, d = x.shape
    steps = rows // batch
    h = _rms(x, g_ref[layer:layer + 1, :]).astype(BF16)
    u = _dot_split(h, win_ref[...])
    u_ref[...] = u
    ub = u.astype(BF16)
    lanes = lambda blk: slice(blk * LANES, (blk + 1) * LANES)

    def project_in(blk):
        bu_ref[blk % 2] = _dot_split(ub[:, lanes(blk)], bm_ref[blk])

    def project_out(blk):
        y_ref[:, lanes(blk)] = _dot_split(st_ref[blk % 2].astype(BF16), cm_ref[blk])

    project_in(0)
    for blk in range(nblock):
        if blk + 1 < nblock:
            project_in(blk + 1)
        if blk >= 1:
            project_out(blk - 1)
        slot = blk % 2
        ar = a_ref[blk, 0:batch, :]
        ai = a_ref[blk, batch:2 * batch, :]
        sr = state_ref[blk, 0:batch, :]
        si = state_ref[blk, batch:2 * batch, :]
        for t in range(steps):
            r = slice(t * batch, (t + 1) * batch)
            sr, si = (ar * sr - ai * si + bu_ref[slot, r, 0:half],
                      ar * si + ai * sr + bu_ref[slot, r, half:2 * half])
            st_ref[slot, r, 0:half] = sr
            st_ref[slot, r, half:2 * half] = si
        state_ref[blk, 0:batch, :] = sr
        state_ref[blk, batch:2 * batch, :] = si
    project_out(nblock - 1)
    y = y_ref[...] + d_ref[layer:layer + 1, :] * u_ref[...]
    act = jax.nn.gelu(y).astype(BF16)
    z = _dot_split(act, wglu_ref[...])
    o_ref[...] = x + z[:, :d] * jax.nn.sigmoid(z[:, d:])


def _s5(x, w, layer, *, batch):
    rows, d = x.shape
    _, nblock, _, width = w["bm"].shape
    half = width // 2
    tr = min(ROW_TILE, rows)
    return pl.pallas_call(
        functools.partial(_s5_kernel, layer=layer, batch=batch, nblock=nblock, half=half),
        grid=(rows // tr,),
        in_specs=[
            pl.BlockSpec((tr, d), lambda i: (i, 0)),
            _resident(w["norm"].shape),
            _resident(w["win"].shape, layer),
            _resident(w["bm"].shape, layer),
            _resident(w["a"].shape, layer),
            _resident(w["cm"].shape, layer),
            _resident(w["d_skip"].shape),
            _resident(w["wglu"].shape, layer),
        ],
        out_specs=pl.BlockSpec((tr, d), lambda i: (i, 0)),
        out_shape=jax.ShapeDtypeStruct((rows, d), F32),
        scratch_shapes=[
            pltpu.VMEM((nblock, 2 * batch, half), F32),
            pltpu.VMEM((tr, d), F32),
            pltpu.VMEM((2, tr, 2 * half), F32),
            pltpu.VMEM((2, tr, 2 * half), F32),
            pltpu.VMEM((tr, d), F32),
        ],
        compiler_params=_params("arbitrary"),
        name="s5_mixer",
    )(x, w["norm"], w["win"], w["bm"], w["a"], w["cm"], w["d_skip"], w["wglu"])


def _s5_weights(ssm_norm, ssm_w_in, lambda_re, lambda_im, log_step, b_re, b_im, c_re, c_im, d_skip, w_glu,
                *, batch):
    nl, groups, nstate = lambda_re.shape
    gpb = GROUPS_PER_BLOCK
    nblock = groups // gpb
    half = gpb * nstate
    merge = lambda a: a.reshape((nl * groups,) + a.shape[2:])
    ar, ai, bbr, bbi = _s5_discretize(merge(lambda_re), merge(lambda_im), merge(log_step),
                                      merge(b_re), merge(b_im))
    eye = jnp.eye(gpb, dtype=F32)

    def in_mat(bb):
        bb = bb.reshape(nl, nblock, gpb, nstate, SSM_GROUP)
        return jnp.einsum("LGgpc,hg->LGhcgp", bb, eye).reshape(nl, nblock, LANES, half)

    def out_mat(cc):
        cc = cc.reshape(nl, nblock, gpb, SSM_GROUP, nstate)
        return jnp.einsum("LGgcp,gh->LGgphc", cc, eye).reshape(nl, nblock, half, LANES)

    bm = jnp.concatenate([in_mat(bbr), in_mat(bbi)], axis=3).astype(BF16)
    cm = jnp.concatenate([out_mat(c_re), out_mat(-c_im)], axis=2).astype(BF16)
    rep = lambda a: jnp.broadcast_to(a.reshape(nl, nblock, 1, half), (nl, nblock, batch, half))
    a_mat = jnp.concatenate([rep(ar), rep(ai)], axis=2)
    return dict(norm=ssm_norm, win=ssm_w_in.astype(BF16), bm=bm, a=a_mat, cm=cm, d_skip=d_skip,
                wglu=w_glu.astype(BF16))


def kernel(x, attn_norm, mix_w_in, cq_norm, ckv_norm, w_uq, w_ukv, q_gain, k_gain, sconv_w, mix_w_out,
           ssm_norm, ssm_w_in, lambda_re, lambda_im, log_step, b_re, b_im, c_re, c_im, d_skip, w_glu,
           ffn_norm, ffn_w_up, ffn_conv_w, ffn_w_down):
    batch, seq, d = x.shape
    depth = ffn_norm.shape[0]
    tables = _rope_tables(seq, batch)
    mix_w = _mixproj_weights(attn_norm, mix_w_in, cq_norm, ckv_norm, w_uq, w_ukv, q_gain, k_gain, sconv_w)
    w_out = mix_w_out.astype(BF16)
    s5_w = _s5_weights(ssm_norm, ssm_w_in, lambda_re, lambda_im, log_step, b_re, b_im, c_re, c_im,
                       d_skip, w_glu, batch=batch)
    ffn_w = _ffn_weights(ffn_norm, ffn_w_up, ffn_conv_w, ffn_w_down)
    xt = x
    for layer in range(depth):
        i = layer // 2
        mixer = None
        if layer % 2 == 0:
            first = layer == 0
            outs = _mixproj(xt, mix_w, i, tables, batch=batch, first=first)
            q, k, v, conv = outs[:4]
            if first:
                xt = outs[4]
            mixer = (_attention(q, k, v, seq=seq, batch=batch), conv, w_out, i)
        else:
            xt = _s5(xt, s5_w, i, batch=batch)
        xt = _ffn(xt, ffn_w, layer, batch=batch, mixer=mixer, last=layer == depth - 1)
    return xt
```

```python
import functools
import math

import jax
import jax.numpy as jnp
from jax import lax
from jax.experimental import pallas as pl
from jax.experimental.pallas import tpu as pltpu

F32 = jnp.float32
BF16 = jnp.bfloat16

EPS = 1e-6
ROPE_THETA = 10000.0
LANES = 128
MXU_TILE = 256
MLA_HEADS = 8
QK_NOPE = 64
QK_ROPE = 32
QK_DIM = QK_NOPE + QK_ROPE
V_DIM = 64
LORA = 256
CONV_CH = 512
SSM_GROUP = 16
GROUPS_PER_BLOCK = LANES // SSM_GROUP
HEAD_LANES = MLA_HEADS * LANES
VMEM_LIMIT = 56 * 1024 * 1024
ROW_TILE = 512
FFN_ROW_TILE = 1024
ATTN_TILE = 1024
FFN_CHUNK = MXU_TILE


def _params(*sem):
    return pltpu.CompilerParams(dimension_semantics=sem, vmem_limit_bytes=VMEM_LIMIT)


def _resident(shape, layer=None):
    if layer is None:
        nd = len(shape)
        return pl.BlockSpec(shape, lambda *_: (0,) * nd, pipeline_mode=pl.Buffered(1))
    nd = len(shape) - 1
    return pl.BlockSpec((None,) + tuple(shape[1:]), lambda *_: (layer,) + (0,) * nd,
                        pipeline_mode=pl.Buffered(1))


def _rms(x, gain):
    return x * lax.rsqrt(jnp.mean(x * x, axis=-1, keepdims=True) + EPS) * gain


def _dot_split(a, w, parts=2):
    rows = a.shape[0]
    step = rows // parts
    return jnp.concatenate(
        [jnp.dot(a[r:r + step], w, preferred_element_type=F32) for r in range(0, rows, step)], axis=0)


def _shift_conv(prev, cur, w):
    rows = cur.shape[0]
    step = prev.shape[0] // 2
    ext = jnp.concatenate([prev, cur], axis=0)
    return w[0:1] * ext[0:rows] + w[1:2] * ext[step:rows + step] + w[2:3] * cur


def _to_time_major(src_ref, dst_ref, batch):
    steps = src_ref.shape[1]
    for b in range(batch):
        for c in range(dst_ref.shape[0]):
            dst_ref[c, pl.ds(b, steps, stride=batch), :] = src_ref[b, :, c * LANES:(c + 1) * LANES]


def _ffn_kernel(*refs, layer, hidden, halo, batch, mix, last):
    refs = list(refs)
    x_ref = refs.pop(0)
    if mix:
        a_ref, cv_ref, wo_ref = refs.pop(0), refs.pop(0), refs.pop(0)
    g_ref, wup_ref, cw_ref, wdn_ref, o_ref, carry_ref = (refs.pop(0) for _ in range(6))
    if mix:
        ail_ref = refs.pop(0)
    if last:
        os_ref = refs.pop(0)

    @pl.when(pl.program_id(0) == 0)
    def _():
        carry_ref[...] = jnp.zeros_like(carry_ref)

    x = x_ref[...]
    rows, d = x.shape
    steps = rows // batch
    nblk = d // LANES
    if mix:
        na = a_ref.shape[1] // batch
        for b in range(batch):
            for c in range(na // LANES):
                col = b * na + c * LANES
                ail_ref[c, pl.ds(b, steps, stride=batch), :] = a_ref[:, col:col + LANES]
        attn = jnp.concatenate([ail_ref[c] for c in range(na // LANES)], axis=1).astype(BF16)
        x = (x + jnp.dot(attn, wo_ref[:na, :], preferred_element_type=F32)
             + jnp.dot(cv_ref[...], wo_ref[na:, :], preferred_element_type=F32))
    h = _rms(x, g_ref[layer:layer + 1, :]).astype(BF16)
    if last:
        for c in range(nblk):
            os_ref[c] = x[:, c * LANES:(c + 1) * LANES]
    else:
        o_ref[...] = x

    def up_project(c0):
        return tuple(_dot_split(h, wup_ref[:, j * hidden + c0:j * hidden + c0 + FFN_CHUNK]) for j in range(2))

    def conv(u, j, c0):
        cols = slice(j * hidden + c0, j * hidden + c0 + FFN_CHUNK)
        prev = carry_ref[:, cols]
        carry_ref[:, cols] = u[rows - halo:, :]
        return _shift_conv(prev, u, cw_ref[:, cols])

    chunks = list(range(0, hidden, FFN_CHUNK))
    u_next = up_project(chunks[0])
    for n, c0 in enumerate(chunks):
        u_gate, u_val = u_next
        if n + 1 < len(chunks):
            u_next = up_project(chunks[n + 1])
        gate = conv(u_gate, 0, c0)
        val = conv(u_val, 1, c0)
        act = (gate * jax.nn.sigmoid(gate) * val).astype(BF16)
        res = jnp.dot(act, wdn_ref[c0:c0 + FFN_CHUNK, :], preferred_element_type=F32)
        if last:
            for c in range(nblk):
                os_ref[c] += res[:, c * LANES:(c + 1) * LANES]
        else:
            o_ref[...] += res
    if last:
        for b in range(batch):
            for c in range(nblk):
                o_ref[b, :, c * LANES:(c + 1) * LANES] = os_ref[c, pl.ds(b, steps, stride=batch), :]


def _ffn(x, w, layer, *, batch, mixer=None, last=False):
    rows, d = x.shape
    hidden = w["wdn"].shape[1]
    halo = 2 * batch
    tr = min(FFN_ROW_TILE, rows)
    ts = tr // batch
    row_spec = lambda n: pl.BlockSpec((tr, n), lambda i: (i, 0))
    operands = [x]
    in_specs = [row_spec(d)]
    scratch = [pltpu.VMEM((halo, 2 * hidden), F32)]
    if mixer is not None:
        attn, conv, w_out, mix_layer = mixer
        operands += [attn, conv, w_out]
        in_specs += [pl.BlockSpec((ts, attn.shape[1]), lambda i: (i, 0)), row_spec(conv.shape[1]),
                     _resident(w_out.shape, mix_layer)]
        scratch.append(pltpu.VMEM((attn.shape[1] // batch // LANES, tr, LANES), F32))
    operands += [w["norm"], w["wup"], w["cw"], w["wdn"]]
    in_specs += [_resident(w["norm"].shape), _resident(w["wup"].shape, layer),
                 _resident(w["cw"].shape, layer), _resident(w["wdn"].shape, layer)]
    if last:
        scratch.append(pltpu.VMEM((d // LANES, tr, LANES), F32))
        out_spec = pl.BlockSpec((batch, ts, d), lambda i: (0, i, 0))
        out_shape = jax.ShapeDtypeStruct((batch, rows // batch, d), F32)
    else:
        out_spec = row_spec(d)
        out_shape = jax.ShapeDtypeStruct((rows, d), F32)
    return pl.pallas_call(
        functools.partial(_ffn_kernel, layer=layer, hidden=hidden, halo=halo, batch=batch,
                          mix=mixer is not None, last=last),
        grid=(rows // tr,),
        in_specs=in_specs,
        out_specs=out_spec,
        out_shape=out_shape,
        scratch_shapes=scratch,
        compiler_params=_params("arbitrary"),
        name="conv_ffn",
    )(*operands)


def _ffn_weights(ffn_norm, ffn_w_up, ffn_conv_w, ffn_w_down):
    return dict(norm=ffn_norm, wup=ffn_w_up.astype(BF16), cw=ffn_conv_w, wdn=ffn_w_down.astype(BF16))


def _head_sumsq(x, ones_ref):
    sq = (x * x).astype(BF16)
    parts = [jnp.dot(sq[:, c:c + MXU_TILE], ones_ref[...], preferred_element_type=F32)
             for c in range(0, x.shape[1], MXU_TILE)]
    return jnp.concatenate(parts, axis=1)


def _mixproj_kernel(*refs, layer, halo, batch, first):
    refs = list(refs)
    (x_ref, g_ref, win_ref, cqg_ref, ckvg_ref, wuq_ref, wukv_ref, gains_ref, ones_ref, scw_ref,
     ct_ref, st_ref, q_ref, k_ref, v_ref, cv_ref) = (refs.pop(0) for _ in range(16))
    if first:
        xtm_ref = refs.pop(0)
    carry_ref, qs_ref, ks_ref, vs_ref = (refs.pop(0) for _ in range(4))
    if first:
        xs_ref = refs.pop(0)

    @pl.when(pl.program_id(0) == 0)
    def _():
        carry_ref[...] = jnp.zeros_like(carry_ref)

    if first:
        _to_time_major(x_ref, xs_ref, batch)
        x = jnp.concatenate([xs_ref[c] for c in range(xs_ref.shape[0])], axis=1)
        xtm_ref[...] = x
    else:
        x = x_ref[...]
    rows = x.shape[0]
    steps = rows // batch
    pick = lambda ref: ref[layer:layer + 1, :]
    h = _rms(x, pick(g_ref)).astype(BF16)
    proj = _dot_split(h, win_ref[...])
    o = 0
    c_q = proj[:, o:o + LORA]; o += LORA
    c_kv = proj[:, o:o + LORA]; o += LORA
    k_rope = proj[:, o:o + LANES]; o += LANES
    k_rope_rot = proj[:, o:o + LANES]; o += LANES
    gate_b = proj[:, o:o + CONV_CH]; o += CONV_CH
    gate_c = proj[:, o:o + CONV_CH]; o += CONV_CH
    conv_in = proj[:, o:o + CONV_CH]

    m = gate_c * conv_in
    prev = carry_ref[...]
    carry_ref[...] = m[rows - halo:, :]
    cv_ref[...] = (gate_b * _shift_conv(prev, m, scw_ref[layer])).astype(BF16)

    qq = jnp.dot(_rms(c_q, pick(cqg_ref)).astype(BF16), wuq_ref[...], preferred_element_type=F32)
    kv = jnp.dot(_rms(c_kv, pick(ckvg_ref)).astype(BF16), wukv_ref[...], preferred_element_type=F32)
    q, q_rot = qq[:, :HEAD_LANES], qq[:, HEAD_LANES:]
    kn, v = kv[:, :HEAD_LANES], kv[:, HEAD_LANES:]
    ct, st = ct_ref[...], st_ref[...]
    gains = gains_ref[layer]
    tq, sq = ct * gains[0:1], st * gains[1:2]
    tk, sk = ct * gains[2:3], st * gains[3:4]
    k_nope_gain, v_ones = gains[4:5], gains[5:6]
    inv_dim = 1.0 / QK_DIM
    q_inv = lax.rsqrt(_head_sumsq(q, ones_ref) * inv_dim + EPS)
    kr_sumsq = jnp.sum(k_rope * k_rope, axis=-1, keepdims=True)
    k_inv = lax.rsqrt((_head_sumsq(kn, ones_ref) + kr_sumsq) * inv_dim + EPS)
    kr = k_rope * tk + k_rope_rot * sk
    for hd in range(MLA_HEADS):
        sl = slice(hd * LANES, (hd + 1) * LANES)
        qs_ref[hd] = (q[:, sl] * tq + q_rot[:, sl] * sq) * q_inv[:, sl]
        ks_ref[hd] = (kn[:, sl] * k_nope_gain + kr) * k_inv[:, sl]
        vs_ref[hd] = v[:, sl] + v_ones
    for src, dst in ((qs_ref, q_ref), (ks_ref, k_ref), (vs_ref, v_ref)):
        for b in range(batch):
            for hd in range(MLA_HEADS):
                col = (b * MLA_HEADS + hd) * LANES
                dst[:, col:col + LANES] = src[hd, pl.ds(b, steps, stride=batch), :].astype(BF16)


def _mixproj(x, w, layer, tables, *, batch, first=False):
    if first:
        _, seq, d = x.shape
        rows = seq * batch
    else:
        rows, d = x.shape
    halo = 2 * batch
    tr = min(ROW_TILE, rows)
    ts = tr // batch
    row_spec = lambda n: pl.BlockSpec((tr, n), lambda i: (i, 0))
    wide_spec = pl.BlockSpec((ts, batch * HEAD_LANES), lambda i: (i, 0))
    wide_shape = jax.ShapeDtypeStruct((rows // batch, batch * HEAD_LANES), BF16)
    x_spec = pl.BlockSpec((batch, ts, d), lambda i: (0, i, 0)) if first else row_spec(d)
    out_specs = [wide_spec, wide_spec, wide_spec, row_spec(CONV_CH)]
    out_shape = [wide_shape, wide_shape, wide_shape, jax.ShapeDtypeStruct((rows, CONV_CH), BF16)]
    scratch = [pltpu.VMEM((halo, CONV_CH), F32)] + [pltpu.VMEM((MLA_HEADS, tr, LANES), F32)] * 3
    if first:
        out_specs.append(row_spec(d))
        out_shape.append(jax.ShapeDtypeStruct((rows, d), F32))
        scratch.append(pltpu.VMEM((d // LANES, tr, LANES), F32))
    return pl.pallas_call(
        functools.partial(_mixproj_kernel, layer=layer, halo=halo, batch=batch, first=first),
        grid=(rows // tr,),
        in_specs=[
            x_spec,
            _resident(w["norm"].shape),
            _resident(w["win"].shape, layer),
            _resident(w["cq_norm"].shape),
            _resident(w["ckv_norm"].shape),
            _resident(w["wuq"].shape, layer),
            _resident(w["wukv"].shape, layer),
            _resident(w["gains"].shape),
            _resident(w["ones_bd"].shape),
            _resident(w["scw"].shape),
            row_spec(LANES), row_spec(LANES),
        ],
        out_specs=out_specs,
        out_shape=out_shape,
        scratch_shapes=scratch,
        compiler_params=_params("arbitrary"),
        name="mix_proj",
    )(x, w["norm"], w["win"], w["cq_norm"], w["ckv_norm"], w["wuq"], w["wukv"], w["gains"],
      w["ones_bd"], w["scw"], *tables)


def _mixproj_weights(attn_norm, mix_w_in, cq_norm, ckv_norm, w_uq, w_ukv, q_gain, k_gain, sconv_w):
    nl, d, _ = mix_w_in.shape
    half = QK_ROPE // 2
    tail = LANES - QK_DIM
    zeros = lambda n: jnp.zeros((nl, d, n), F32)
    o1, o2 = 2 * LORA, 2 * LORA + QK_ROPE
    w_kr = mix_w_in[:, :, o1:o2]
    win = jnp.concatenate(
        [mix_w_in[:, :, :o1],
         zeros(QK_NOPE), w_kr, zeros(tail),
         zeros(QK_NOPE), -w_kr[:, :, half:], w_kr[:, :, :half], zeros(tail),
         mix_w_in[:, :, o2:]], axis=2).astype(BF16)
    wq = w_uq.reshape(nl, LORA, MLA_HEADS, QK_DIM)
    pad = lambda a: jnp.pad(a, ((0, 0), (0, 0), (0, 0), (0, LANES - a.shape[3]))).reshape(nl, LORA, HEAD_LANES)
    wq_rot = jnp.concatenate([jnp.zeros_like(wq[..., :QK_NOPE]), -wq[..., QK_NOPE + half:],
                              wq[..., QK_NOPE:QK_NOPE + half]], axis=3)
    wuq = jnp.concatenate([pad(wq), pad(wq_rot)], axis=2).astype(BF16)
    wkv = w_ukv.reshape(nl, LORA, MLA_HEADS, QK_NOPE + V_DIM)
    wukv = jnp.concatenate([pad(wkv[..., :QK_NOPE]), pad(wkv[..., QK_NOPE:])], axis=2).astype(BF16)

    def rope_gains(g):
        z = jnp.zeros((nl, tail), F32)
        return (jnp.concatenate([g, z], axis=1),
                jnp.concatenate([jnp.zeros((nl, QK_NOPE), F32), g[:, QK_NOPE + half:],
                                 g[:, QK_NOPE:QK_NOPE + half], z], axis=1))

    gq = rope_gains(q_gain * (QK_DIM ** -0.5 * math.log2(math.e)))
    gk = rope_gains(k_gain)
    k_nope_gain = jnp.concatenate([k_gain[:, :QK_NOPE], jnp.zeros((nl, LANES - QK_NOPE), F32)], axis=1)
    v_ones = jnp.broadcast_to((jnp.arange(LANES) == V_DIM).astype(F32), (nl, LANES))
    zero = jnp.zeros((nl, LANES), F32)
    gains = jnp.stack([*gq, *gk, k_nope_gain, v_ones, zero, zero], axis=1)
    lane_head = jnp.arange(MXU_TILE) // LANES
    ones_bd = (lane_head[:, None] == lane_head[None, :]).astype(BF16)
    return dict(norm=attn_norm, win=win, cq_norm=cq_norm, ckv_norm=ckv_norm, wuq=wuq, wukv=wukv,
                gains=gains, ones_bd=ones_bd, scw=sconv_w)


def _rope_tables(seq, batch):
    inv_freq = 1.0 / (ROPE_THETA ** (jnp.arange(0, QK_ROPE, 2, dtype=F32) / QK_ROPE))
    ang = jnp.arange(seq, dtype=F32)[:, None] * inv_freq[None, :]
    cos, sin = jnp.cos(ang), jnp.sin(ang)
    z = lambda n: jnp.zeros((seq, n), F32)
    tail = LANES - QK_DIM
    ct = jnp.concatenate([jnp.ones((seq, QK_NOPE), F32), cos, cos, z(tail)], axis=1)
    st = jnp.concatenate([z(QK_NOPE), sin, sin, z(tail)], axis=1)
    return tuple(jnp.repeat(t, batch, axis=0) for t in (ct, st))


def _attn_kernel(q_ref, k_ref, v_ref, o_ref, sa_ref, sb_ref, m_ref, acc_ref, *, tq):
    tk = tq // 2
    qi = pl.program_id(2)
    heads = [slice(hh * LANES, (hh + 1) * LANES) for hh in range(2)]
    nt = (((1,), (1,)), ((), ()))

    def scores(kb, dst, r0=0):
        start = pl.multiple_of(kb * tk, tk)
        for hh, sl in enumerate(heads):
            dst[hh, 0:tq - r0, :] = lax.dot_general(q_ref[r0:, sl], k_ref[pl.ds(start, tk), sl], nt,
                                                    preferred_element_type=F32)

    def update(kb, src, r0=0, diagonal=False):
        start = pl.multiple_of(kb * tk, tk)
        n = tq - r0
        for hh, sl in enumerate(heads):
            s = src[hh, 0:n, :]
            if diagonal:
                row = lax.broadcasted_iota(jnp.int32, (n, tk), 0)
                col = lax.broadcasted_iota(jnp.int32, (n, tk), 1)
                s = jnp.where(col <= row, s, -jnp.inf)
            m_old = m_ref[hh, r0:, :]
            m_new = jnp.maximum(m_old, jnp.max(s, axis=-1, keepdims=True))
            p = jnp.concatenate(
                [jnp.exp2((s[:, c:c + LANES] - m_new).astype(BF16)) for c in range(0, tk, LANES)], axis=1)
            pv = jnp.dot(p, v_ref[pl.ds(start, tk), sl], preferred_element_type=F32)
            acc_ref[hh, r0:, :] = jnp.exp2(m_old - m_new) * acc_ref[hh, r0:, :] + pv
            m_ref[hh, r0:, :] = m_new

    m_ref[...] = jnp.full(m_ref.shape, -jnp.inf, F32)
    acc_ref[...] = jnp.zeros(acc_ref.shape, F32)
    scores(0, sa_ref)

    def body(j, _):
        scores(2 * j + 1, sb_ref)
        update(2 * j, sa_ref)
        scores(2 * j + 2, sa_ref)
        update(2 * j + 1, sb_ref)
        return 0

    lax.fori_loop(0, qi, body, 0)
    scores(2 * qi + 1, sb_ref, r0=tk)
    update(2 * qi, sa_ref, diagonal=True)
    update(2 * qi + 1, sb_ref, r0=tk, diagonal=True)
    lane = lax.broadcasted_iota(jnp.int32, (1, LANES), 1)
    outs = []
    for hh in range(2):
        acc = acc_ref[hh]
        outs.append(acc / acc[:, V_DIM:V_DIM + 1])
    o_ref[...] = jnp.where(lane < V_DIM, outs[0], pltpu.roll(outs[1], V_DIM, 1))


def _attention(q, k, v, *, seq, batch):
    tq = min(ATTN_TILE, seq)
    pairs = MLA_HEADS // 2
    col = lambda b, j, i: b * pairs + j
    return pl.pallas_call(
        functools.partial(_attn_kernel, tq=tq),
        grid=(batch, pairs, seq // tq),
        in_specs=[
            pl.BlockSpec((tq, 2 * LANES), lambda b, j, i: (i, col(b, j, i))),
            pl.BlockSpec((seq, 2 * LANES), lambda b, j, i: (0, col(b, j, i))),
            pl.BlockSpec((seq, 2 * LANES), lambda b, j, i: (0, col(b, j, i))),
        ],
        out_specs=pl.BlockSpec((tq, LANES), lambda b, j, i: (i, col(b, j, i))),
        out_shape=jax.ShapeDtypeStruct((seq, batch * MLA_HEADS * V_DIM), F32),
        scratch_shapes=[
            pltpu.VMEM((2, tq, tq // 2), F32),
            pltpu.VMEM((2, tq, tq // 2), F32),
            pltpu.VMEM((2, tq, LANES), F32),
            pltpu.VMEM((2, tq, LANES), F32),
        ],
        compiler_params=_params("arbitrary", "arbitrary", "arbitrary"),
        name="causal_attention",
    )(q, k, v)


def _s5_discretize_kernel(lr_ref, li_ref, dt_ref, lre_ref, lie_ref, dte_ref, bre_ref, bim_ref,
                          ar_ref, ai_ref, bbr_ref, bbi_ref):
    def zoh(lr, li, dt):
        mag = jnp.exp(lr * dt)
        ar, ai = mag * jnp.cos(li * dt), mag * jnp.sin(li * dt)
        nr, ni = ar - 1.0, ai
        den = lr * lr + li * li
        return ar, ai, (nr * lr + ni * li) / den, (ni * lr - nr * li) / den

    ar, ai, _, _ = zoh(lr_ref[...], li_ref[...], dt_ref[...])
    ar_ref[...] = ar
    ai_ref[...] = ai
    _, _, zr, zi = zoh(lre_ref[...], lie_ref[...], dte_ref[...])
    br, bi = bre_ref[...], bim_ref[...]
    bbr_ref[...] = zr * br - zi * bi
    bbi_ref[...] = zr * bi + zi * br


def _s5_discretize(lambda_re, lambda_im, log_step, b_re, b_im):
    g, p = lambda_re.shape
    c = b_re.shape[-1]
    dt = jnp.broadcast_to(jnp.exp(log_step)[:, None], (g, p))
    expand = lambda a: jnp.repeat(a, c, axis=1)
    flat = lambda a: a.reshape(g, p * c)
    out = pl.pallas_call(
        _s5_discretize_kernel,
        out_shape=[jax.ShapeDtypeStruct((g, p), F32)] * 2 + [jax.ShapeDtypeStruct((g, p * c), F32)] * 2,
        name="s5_discretize",
    )(lambda_re, lambda_im, dt, expand(lambda_re), expand(lambda_im), expand(dt), flat(b_re), flat(b_im))
    ar, ai, bbr, bbi = out
    return ar, ai, bbr.reshape(g, p, c), bbi.reshape(g, p, c)


def _s5_kernel(x_ref, g_ref, win_ref, bm_ref, a_ref, cm_ref, d_ref, wglu_ref, o_ref,
               state_ref, u_ref, bu_ref, st_ref, y_ref, *, layer, batch, nblock, half):
    @pl.when(pl.program_id(0) == 0)
    def _():
        state_ref[...] = jnp.zeros_like(state_ref)

    x = x_ref[...]
    rows, d = x.shape
    steps = rows // batch
    h = _rms(x, g_ref[layer:layer + 1, :]).astype(BF16)
    u = _dot_split(h, win_ref[...])
    u_ref[...] = u
    ub = u.astype(BF16)
    lanes = lambda blk: slice(blk * LANES, (blk + 1) * LANES)

    def project_in(blk):
        bu_ref[blk % 2] = _dot_split(ub[:, lanes(blk)], bm_ref[blk])

    def project_out(blk):
        y_ref[:, lanes(blk)] = _dot_split(st_ref[blk % 2].astype(BF16), cm_ref[blk])

    project_in(0)
    for blk in range(nblock):
        if blk + 1 < nblock:
            project_in(blk + 1)
        if blk >= 1:
            project_out(blk - 1)
        slot = blk % 2
        ar = a_ref[blk, 0:batch, :]
        ai = a_ref[blk, batch:2 * batch, :]
        sr = state_ref[blk, 0:batch, :]
        si = state_ref[blk, batch:2 * batch, :]
        for t in range(steps):
            r = slice(t * batch, (t + 1) * batch)
            sr, si = (ar * sr - ai * si + bu_ref[slot, r, 0:half],
                      ar * si + ai * sr + bu_ref[slot, r, half:2 * half])
            st_ref[slot, r, 0:half] = sr
            st_ref[slot, r, half:2 * half] = si
        state_ref[blk, 0:batch, :] = sr
        state_ref[blk, batch:2 * batch, :] = si
    project_out(nblock - 1)
    y = y_ref[...] + d_ref[layer:layer + 1, :] * u_ref[...]
    act = jax.nn.gelu(y).astype(BF16)
    z = _dot_split(act, wglu_ref[...])
    o_ref[...] = x + z[:, :d] * jax.nn.sigmoid(z[:, d:])


def _s5(x, w, layer, *, batch):
    rows, d = x.shape
    _, nblock, _, width = w["bm"].shape
    half = width // 2
    tr = min(ROW_TILE, rows)
    return pl.pallas_call(
        functools.partial(_s5_kernel, layer=layer, batch=batch, nblock=nblock, half=half),
        grid=(rows // tr,),
        in_specs=[
            pl.BlockSpec((tr, d), lambda i: (i, 0)),
            _resident(w["norm"].shape),
            _resident(w["win"].shape, layer),
            _resident(w["bm"].shape, layer),
            _resident(w["a"].shape, layer),
            _resident(w["cm"].shape, layer),
            _resident(w["d_skip"].shape),
            _resident(w["wglu"].shape, layer),
        ],
        out_specs=pl.BlockSpec((tr, d), lambda i: (i, 0)),
        out_shape=jax.ShapeDtypeStruct((rows, d), F32),
        scratch_shapes=[
            pltpu.VMEM((nblock, 2 * batch, half), F32),
            pltpu.VMEM((tr, d), F32),
            pltpu.VMEM((2, tr, 2 * half), F32),
            pltpu.VMEM((2, tr, 2 * half), F32),
            pltpu.VMEM((tr, d), F32),
        ],
        compiler_params=_params("arbitrary"),
        name="s5_mixer",
    )(x, w["norm"], w["win"], w["bm"], w["a"], w["cm"], w["d_skip"], w["wglu"])


def _s5_weights(ssm_norm, ssm_w_in, lambda_re, lambda_im, log_step, b_re, b_im, c_re, c_im, d_skip, w_glu,
                *, batch):
    nl, groups, nstate = lambda_re.shape
    gpb = GROUPS_PER_BLOCK
    nblock = groups // gpb
    half = gpb * nstate
    merge = lambda a: a.reshape((nl * groups,) + a.shape[2:])
    ar, ai, bbr, bbi = _s5_discretize(merge(lambda_re), merge(lambda_im), merge(log_step),
                                      merge(b_re), merge(b_im))
    eye = jnp.eye(gpb, dtype=F32)

    def in_mat(bb):
        bb = bb.reshape(nl, nblock, gpb, nstate, SSM_GROUP)
        return jnp.einsum("LGgpc,hg->LGhcgp", bb, eye).reshape(nl, nblock, LANES, half)

    def out_mat(cc):
        cc = cc.reshape(nl, nblock, gpb, SSM_GROUP, nstate)
        return jnp.einsum("LGgcp,gh->LGgphc", cc, eye).reshape(nl, nblock, half, LANES)

    bm = jnp.concatenate([in_mat(bbr), in_mat(bbi)], axis=3).astype(BF16)
    cm = jnp.concatenate([out_mat(c_re), out_mat(-c_im)], axis=2).astype(BF16)
    rep = lambda a: jnp.broadcast_to(a.reshape(nl, nblock, 1, half), (nl, nblock, batch, half))
    a_mat = jnp.concatenate([rep(ar), rep(ai)], axis=2)
    return dict(norm=ssm_norm, win=ssm_w_in.astype(BF16), bm=bm, a=a_mat, cm=cm, d_skip=d_skip,
                wglu=w_glu.astype(BF16))


def kernel(x, attn_norm, mix_w_in, cq_norm, ckv_norm, w_uq, w_ukv, q_gain, k_gain, sconv_w, mix_w_out,
           ssm_norm, ssm_w_in, lambda_re, lambda_im, log_step, b_re, b_im, c_re, c_im, d_skip, w_glu,
           ffn_norm, ffn_w_up, ffn_conv_w, ffn_w_down):
    batch, seq, d = x.shape
    depth = ffn_norm.shape[0]
    tables = _rope_tables(seq, batch)
    mix_w = _mixproj_weights(attn_norm, mix_w_in, cq_norm, ckv_norm, w_uq, w_ukv, q_gain, k_gain, sconv_w)
    w_out = mix_w_out.astype(BF16)
    s5_w = _s5_weights(ssm_norm, ssm_w_in, lambda_re, lambda_im, log_step, b_re, b_im, c_re, c_im,
                       d_skip, w_glu, batch=batch)
    ffn_w = _ffn_weights(ffn_norm, ffn_w_up, ffn_conv_w, ffn_w_down)
    xt = x
    for layer in range(depth):
        i = layer // 2
        mixer = None
        if layer % 2 == 0:
            first = layer == 0
            outs = _mixproj(xt, mix_w, i, tables, batch=batch, first=first)
            q, k, v, conv = outs[:4]
            if first:
                xt = outs[4]
            mixer = (_attention(q, k, v, seq=seq, batch=batch), conv, w_out, i)
        else:
            xt = _s5(xt, s5_w, i, batch=batch)
        xt = _ffn(xt, ffn_w, layer, batch=batch, mixer=mixer, last=layer == depth - 1)
    return xt
```

```python
import functools
import math

import jax
import jax.numpy as jnp
from jax import lax
from jax.experimental import pallas as pl
from jax.experimental.pallas import tpu as pltpu

F32 = jnp.float32
BF16 = jnp.bfloat16

EPS = 1e-6
ROPE_THETA = 10000.0
LANES = 128
MXU_TILE = 256
MLA_HEADS = 8
QK_NOPE = 64
QK_ROPE = 32
QK_DIM = QK_NOPE + QK_ROPE
V_DIM = 64
LORA = 256
CONV_CH = 512
SSM_GROUP = 16
GROUPS_PER_BLOCK = LANES // SSM_GROUP
HEAD_LANES = MLA_HEADS * LANES
VMEM_LIMIT = 56 * 1024 * 1024
ROW_TILE = 512
FFN_ROW_TILE = 1024
ATTN_TILE = 1024
FFN_CHUNK = MXU_TILE


def _params(*sem):
    return pltpu.CompilerParams(dimension_semantics=sem, vmem_limit_bytes=VMEM_LIMIT)


def _resident(shape, layer=None):
    if layer is None:
        nd = len(shape)
        return pl.BlockSpec(shape, lambda *_: (0,) * nd, pipeline_mode=pl.Buffered(1))
    nd = len(shape) - 1
    return pl.BlockSpec((None,) + tuple(shape[1:]), lambda *_: (layer,) + (0,) * nd,
                        pipeline_mode=pl.Buffered(1))


def _rms(x, gain):
    return x * lax.rsqrt(jnp.mean(x * x, axis=-1, keepdims=True) + EPS) * gain


def _dot_split(a, w, parts=2):
    rows = a.shape[0]
    step = rows // parts
    return jnp.concatenate(
        [jnp.dot(a[r:r + step], w, preferred_element_type=F32) for r in range(0, rows, step)], axis=0)


def _shift_conv(prev, cur, w):
    rows = cur.shape[0]
    step = prev.shape[0] // 2
    ext = jnp.concatenate([prev, cur], axis=0)
    return w[0:1] * ext[0:rows] + w[1:2] * ext[step:rows + step] + w[2:3] * cur


def _to_time_major(src_ref, dst_ref, batch):
    steps = src_ref.shape[1]
    for b in range(batch):
        for c in range(dst_ref.shape[0]):
            dst_ref[c, pl.ds(b, steps, stride=batch), :] = src_ref[b, :, c * LANES:(c + 1) * LANES]


def _ffn_kernel(*refs, layer, hidden, halo, batch, mix, last):
    refs = list(refs)
    x_ref = refs.pop(0)
    if mix:
        a_ref, cv_ref, wo_ref = refs.pop(0), refs.pop(0), refs.pop(0)
    g_ref, wup_ref, cw_ref, wdn_ref, o_ref, carry_ref = (refs.pop(0) for _ in range(6))
    if mix:
        ail_ref = refs.pop(0)
    if last:
        os_ref = refs.pop(0)

    @pl.when(pl.program_id(0) == 0)
    def _():
        carry_ref[...] = jnp.zeros_like(carry_ref)

    x = x_ref[...]
    rows, d = x.shape
    steps = rows // batch
    nblk = d // LANES
    if mix:
        npair = a_ref.shape[1]
        na = npair * LANES
        for b in range(batch):
            for c in range(npair):
                ail_ref[c, pl.ds(b, steps, stride=batch), :] = a_ref[b, c]
        attn = jnp.concatenate([ail_ref[c] for c in range(npair)], axis=1).astype(BF16)
        x = (x + jnp.dot(attn, wo_ref[:na, :], preferred_element_type=F32)
             + jnp.dot(cv_ref[...], wo_ref[na:, :], preferred_element_type=F32))
    h = _rms(x, g_ref[layer:layer + 1, :]).astype(BF16)
    if last:
        for c in range(nblk):
            os_ref[c] = x[:, c * LANES:(c + 1) * LANES]
    else:
        o_ref[...] = x

    def up_project(c0):
        return tuple(_dot_split(h, wup_ref[:, j * hidden + c0:j * hidden + c0 + FFN_CHUNK]) for j in range(2))

    def conv(u, j, c0):
        cols = slice(j * hidden + c0, j * hidden + c0 + FFN_CHUNK)
        prev = carry_ref[:, cols]
        carry_ref[:, cols] = u[rows - halo:, :]
        return _shift_conv(prev, u, cw_ref[:, cols])

    chunks = list(range(0, hidden, FFN_CHUNK))
    u_next = up_project(chunks[0])
    for n, c0 in enumerate(chunks):
        u_gate, u_val = u_next
        if n + 1 < len(chunks):
            u_next = up_project(chunks[n + 1])
        gate = conv(u_gate, 0, c0)
        val = conv(u_val, 1, c0)
        act = (gate * jax.nn.sigmoid(gate) * val).astype(BF16)
        res = jnp.dot(act, wdn_ref[c0:c0 + FFN_CHUNK, :], preferred_element_type=F32)
        if last:
            for c in range(nblk):
                os_ref[c] += res[:, c * LANES:(c + 1) * LANES]
        else:
            o_ref[...] += res
    if last:
        for b in range(batch):
            for c in range(nblk):
                o_ref[b, :, c * LANES:(c + 1) * LANES] = os_ref[c, pl.ds(b, steps, stride=batch), :]


def _ffn(x, w, layer, *, batch, mixer=None, last=False):
    rows, d = x.shape
    hidden = w["wdn"].shape[1]
    halo = 2 * batch
    tr = min(FFN_ROW_TILE, rows)
    ts = tr // batch
    row_spec = lambda n: pl.BlockSpec((tr, n), lambda i: (i, 0))
    operands = [x]
    in_specs = [row_spec(d)]
    scratch = [pltpu.VMEM((halo, 2 * hidden), F32)]
    if mixer is not None:
        attn, conv, w_out, mix_layer = mixer
        npair = attn.shape[1]
        operands += [attn, conv, w_out]
        in_specs += [pl.BlockSpec((batch, npair, ts, LANES), lambda i: (0, 0, i, 0)), row_spec(conv.shape[1]),
                     _resident(w_out.shape, mix_layer)]
        scratch.append(pltpu.VMEM((npair, tr, LANES), F32))
    operands += [w["norm"], w["wup"], w["cw"], w["wdn"]]
    in_specs += [_resident(w["norm"].shape), _resident(w["wup"].shape, layer),
                 _resident(w["cw"].shape, layer), _resident(w["wdn"].shape, layer)]
    if last:
        scratch.append(pltpu.VMEM((d // LANES, tr, LANES), F32))
        out_spec = pl.BlockSpec((batch, ts, d), lambda i: (0, i, 0))
        out_shape = jax.ShapeDtypeStruct((batch, rows // batch, d), F32)
    else:
        out_spec = row_spec(d)
        out_shape = jax.ShapeDtypeStruct((rows, d), F32)
    return pl.pallas_call(
        functools.partial(_ffn_kernel, layer=layer, hidden=hidden, halo=halo, batch=batch,
                          mix=mixer is not None, last=last),
        grid=(rows // tr,),
        in_specs=in_specs,
        out_specs=out_spec,
        out_shape=out_shape,
        scratch_shapes=scratch,
        compiler_params=_params("arbitrary"),
        name="conv_ffn",
    )(*operands)


def _ffn_weights(ffn_norm, ffn_w_up, ffn_conv_w, ffn_w_down):
    return dict(norm=ffn_norm, wup=ffn_w_up.astype(BF16), cw=ffn_conv_w, wdn=ffn_w_down.astype(BF16))


def _head_sumsq(x, ones_ref):
    sq = (x * x).astype(BF16)
    parts = [jnp.dot(sq[:, c:c + MXU_TILE], ones_ref[...], preferred_element_type=F32)
             for c in range(0, x.shape[1], MXU_TILE)]
    return jnp.concatenate(parts, axis=1)


def _mixproj_kernel(*refs, layer, halo, batch, first):
    refs = list(refs)
    (x_ref, g_ref, win_ref, cqg_ref, ckvg_ref, wuq_ref, wukv_ref, gains_ref, ones_ref, scw_ref,
     ct_ref, st_ref, q_ref, k_ref, v_ref, cv_ref) = (refs.pop(0) for _ in range(16))
    if first:
        xtm_ref = refs.pop(0)
    carry_ref, qs_ref, ks_ref, vs_ref = (refs.pop(0) for _ in range(4))
    if first:
        xs_ref = refs.pop(0)

    @pl.when(pl.program_id(0) == 0)
    def _():
        carry_ref[...] = jnp.zeros_like(carry_ref)

    if first:
        _to_time_major(x_ref, xs_ref, batch)
        x = jnp.concatenate([xs_ref[c] for c in range(xs_ref.shape[0])], axis=1)
        xtm_ref[...] = x
    else:
        x = x_ref[...]
    rows = x.shape[0]
    steps = rows // batch
    pick = lambda ref: ref[layer:layer + 1, :]
    h = _rms(x, pick(g_ref)).astype(BF16)
    proj = jnp.dot(h, win_ref[...], preferred_element_type=F32)
    o = 0
    c_q = proj[:, o:o + LORA]; o += LORA
    c_kv = proj[:, o:o + LORA]; o += LORA
    k_rope = proj[:, o:o + LANES]; o += LANES
    k_rope_rot = proj[:, o:o + LANES]; o += LANES
    gate_b = proj[:, o:o + CONV_CH]; o += CONV_CH
    gate_c = proj[:, o:o + CONV_CH]; o += CONV_CH
    conv_in = proj[:, o:o + CONV_CH]

    m = gate_c * conv_in
    prev = carry_ref[...]
    carry_ref[...] = m[rows - halo:, :]
    cv_ref[...] = (gate_b * _shift_conv(prev, m, scw_ref[layer])).astype(BF16)

    qq = jnp.dot(_rms(c_q, pick(cqg_ref)).astype(BF16), wuq_ref[...], preferred_element_type=F32)
    kv = jnp.dot(_rms(c_kv, pick(ckvg_ref)).astype(BF16), wukv_ref[...], preferred_element_type=F32)
    q, q_rot = qq[:, :HEAD_LANES], qq[:, HEAD_LANES:]
    kn, v = kv[:, :HEAD_LANES], kv[:, HEAD_LANES:]
    ct, st = ct_ref[...], st_ref[...]
    gains = gains_ref[layer]
    tq, sq = ct * gains[0:1], st * gains[1:2]
    tk, sk = ct * gains[2:3], st * gains[3:4]
    k_nope_gain, v_ones = gains[4:5], gains[5:6]
    inv_dim = 1.0 / QK_DIM
    q_inv = lax.rsqrt(_head_sumsq(q, ones_ref) * inv_dim + EPS)
    kr_sumsq = jnp.sum(k_rope * k_rope, axis=-1, keepdims=True)
    k_inv = lax.rsqrt((_head_sumsq(kn, ones_ref) + kr_sumsq) * inv_dim + EPS)
    kr = k_rope * tk + k_rope_rot * sk
    for hd in range(MLA_HEADS):
        sl = slice(hd * LANES, (hd + 1) * LANES)
        qs_ref[hd] = (q[:, sl] * tq + q_rot[:, sl] * sq) * q_inv[:, sl]
        ks_ref[hd] = (kn[:, sl] * k_nope_gain + kr) * k_inv[:, sl]
        vs_ref[hd] = v[:, sl] + v_ones
    for src, dst in ((qs_ref, q_ref), (ks_ref, k_ref), (vs_ref, v_ref)):
        for b in range(batch):
            for hd in range(MLA_HEADS):
                col = (hd % 2) * LANES
                dst[b, hd // 2, :, col:col + LANES] = src[hd, pl.ds(b, steps, stride=batch), :].astype(BF16)


def _mixproj(x, w, layer, tables, *, batch, first=False):
    if first:
        _, seq, d = x.shape
        rows = seq * batch
    else:
        rows, d = x.shape
    halo = 2 * batch
    tr = min(ROW_TILE, rows)
    ts = tr // batch
    row_spec = lambda n: pl.BlockSpec((tr, n), lambda i: (i, 0))
    pairs = MLA_HEADS // 2
    wide_spec = pl.BlockSpec((batch, pairs, ts, 2 * LANES), lambda i: (0, 0, i, 0))
    wide_shape = jax.ShapeDtypeStruct((batch, pairs, rows // batch, 2 * LANES), BF16)
    x_spec = pl.BlockSpec((batch, ts, d), lambda i: (0, i, 0)) if first else row_spec(d)
    out_specs = [wide_spec, wide_spec, wide_spec, row_spec(CONV_CH)]
    out_shape = [wide_shape, wide_shape, wide_shape, jax.ShapeDtypeStruct((rows, CONV_CH), BF16)]
    scratch = [pltpu.VMEM((halo, CONV_CH), F32)] + [pltpu.VMEM((MLA_HEADS, tr, LANES), F32)] * 3
    if first:
        out_specs.append(row_spec(d))
        out_shape.append(jax.ShapeDtypeStruct((rows, d), F32))
        scratch.append(pltpu.VMEM((d // LANES, tr, LANES), F32))
    return pl.pallas_call(
        functools.partial(_mixproj_kernel, layer=layer, halo=halo, batch=batch, first=first),
        grid=(rows // tr,),
        in_specs=[
            x_spec,
            _resident(w["norm"].shape),
            _resident(w["win"].shape, layer),
            _resident(w["cq_norm"].shape),
            _resident(w["ckv_norm"].shape),
            _resident(w["wuq"].shape, layer),
            _resident(w["wukv"].shape, layer),
            _resident(w["gains"].shape),
            _resident(w["ones_bd"].shape),
            _resident(w["scw"].shape),
            row_spec(LANES), row_spec(LANES),
        ],
        out_specs=out_specs,
        out_shape=out_shape,
        scratch_shapes=scratch,
        compiler_params=_params("arbitrary"),
        name="mix_proj",
    )(x, w["norm"], w["win"], w["cq_norm"], w["ckv_norm"], w["wuq"], w["wukv"], w["gains"],
      w["ones_bd"], w["scw"], *tables)


def _mixproj_weights(attn_norm, mix_w_in, cq_norm, ckv_norm, w_uq, w_ukv, q_gain, k_gain, sconv_w):
    nl, d, _ = mix_w_in.shape
    half = QK_ROPE // 2
    tail = LANES - QK_DIM
    zeros = lambda n: jnp.zeros((nl, d, n), F32)
    o1, o2 = 2 * LORA, 2 * LORA + QK_ROPE
    w_kr = mix_w_in[:, :, o1:o2]
    win = jnp.concatenate(
        [mix_w_in[:, :, :o1],
         zeros(QK_NOPE), w_kr, zeros(tail),
         zeros(QK_NOPE), -w_kr[:, :, half:], w_kr[:, :, :half], zeros(tail),
         mix_w_in[:, :, o2:]], axis=2).astype(BF16)
    wq = w_uq.reshape(nl, LORA, MLA_HEADS, QK_DIM)
    pad = lambda a: jnp.pad(a, ((0, 0), (0, 0), (0, 0), (0, LANES - a.shape[3]))).reshape(nl, LORA, HEAD_LANES)
    wq_rot = jnp.concatenate([jnp.zeros_like(wq[..., :QK_NOPE]), -wq[..., QK_NOPE + half:],
                              wq[..., QK_NOPE:QK_NOPE + half]], axis=3)
    wuq = jnp.concatenate([pad(wq), pad(wq_rot)], axis=2).astype(BF16)
    wkv = w_ukv.reshape(nl, LORA, MLA_HEADS, QK_NOPE + V_DIM)
    wukv = jnp.concatenate([pad(wkv[..., :QK_NOPE]), pad(wkv[..., QK_NOPE:])], axis=2).astype(BF16)

    def rope_gains(g):
        z = jnp.zeros((nl, tail), F32)
        return (jnp.concatenate([g, z], axis=1),
                jnp.concatenate([jnp.zeros((nl, QK_NOPE), F32), g[:, QK_NOPE + half:],
                                 g[:, QK_NOPE:QK_NOPE + half], z], axis=1))

    gq = rope_gains(q_gain * (QK_DIM ** -0.5 * math.log2(math.e)))
    gk = rope_gains(k_gain)
    k_nope_gain = jnp.concatenate([k_gain[:, :QK_NOPE], jnp.zeros((nl, LANES - QK_NOPE), F32)], axis=1)
    v_ones = jnp.broadcast_to((jnp.arange(LANES) == V_DIM).astype(F32), (nl, LANES))
    zero = jnp.zeros((nl, LANES), F32)
    gains = jnp.stack([*gq, *gk, k_nope_gain, v_ones, zero, zero], axis=1)
    lane_head = jnp.arange(MXU_TILE) // LANES
    ones_bd = (lane_head[:, None] == lane_head[None, :]).astype(BF16)
    return dict(norm=attn_norm, win=win, cq_norm=cq_norm, ckv_norm=ckv_norm, wuq=wuq, wukv=wukv,
                gains=gains, ones_bd=ones_bd, scw=sconv_w)


def _rope_tables(seq, batch):
    inv_freq = 1.0 / (ROPE_THETA ** (jnp.arange(0, QK_ROPE, 2, dtype=F32) / QK_ROPE))
    ang = jnp.arange(seq, dtype=F32)[:, None] * inv_freq[None, :]
    cos, sin = jnp.cos(ang), jnp.sin(ang)
    z = lambda n: jnp.zeros((seq, n), F32)
    tail = LANES - QK_DIM
    ct = jnp.concatenate([jnp.ones((seq, QK_NOPE), F32), cos, cos, z(tail)], axis=1)
    st = jnp.concatenate([z(QK_NOPE), sin, sin, z(tail)], axis=1)
    return tuple(jnp.repeat(t, batch, axis=0) for t in (ct, st))


def _attn_kernel(q_ref, k_ref, v_ref, o_ref, sa_ref, sb_ref, m_ref, acc_ref, *, tq):
    tk = tq // 2
    qi = pl.program_id(2)
    heads = [slice(hh * LANES, (hh + 1) * LANES) for hh in range(2)]
    nt = (((1,), (1,)), ((), ()))

    def scores(kb, dst, r0=0):
        start = pl.multiple_of(kb * tk, tk)
        for hh, sl in enumerate(heads):
            dst[hh, 0:tq - r0, :] = lax.dot_general(q_ref[r0:, sl], k_ref[pl.ds(start, tk), sl], nt,
                                                    preferred_element_type=F32)

    def update(kb, src, r0=0, diagonal=False):
        start = pl.multiple_of(kb * tk, tk)
        n = tq - r0
        for hh, sl in enumerate(heads):
            s = src[hh, 0:n, :]
            if diagonal:
                row = lax.broadcasted_iota(jnp.int32, (n, tk), 0)
                col = lax.broadcasted_iota(jnp.int32, (n, tk), 1)
                s = jnp.where(col <= row, s, -jnp.inf)
            m_old = m_ref[hh, r0:, :]
            m_new = jnp.maximum(m_old, jnp.max(s, axis=-1, keepdims=True))
            p = jnp.concatenate(
                [jnp.exp2((s[:, c:c + LANES] - m_new).astype(BF16)) for c in range(0, tk, LANES)], axis=1)
            pv = jnp.dot(p, v_ref[pl.ds(start, tk), sl], preferred_element_type=F32)
            acc_ref[hh, r0:, :] = jnp.exp2(m_old - m_new) * acc_ref[hh, r0:, :] + pv
            m_ref[hh, r0:, :] = m_new

    m_ref[...] = jnp.full(m_ref.shape, -jnp.inf, F32)
    acc_ref[...] = jnp.zeros(acc_ref.shape, F32)
    scores(0, sa_ref)

    def body(j, _):
        scores(2 * j + 1, sb_ref)
        update(2 * j, sa_ref)
        scores(2 * j + 2, sa_ref)
        update(2 * j + 1, sb_ref)
        return 0

    lax.fori_loop(0, qi, body, 0)
    scores(2 * qi + 1, sb_ref, r0=tk)
    update(2 * qi, sa_ref, diagonal=True)
    update(2 * qi + 1, sb_ref, r0=tk, diagonal=True)
    lane = lax.broadcasted_iota(jnp.int32, (1, LANES), 1)
    outs = []
    for hh in range(2):
        acc = acc_ref[hh]
        outs.append(acc / acc[:, V_DIM:V_DIM + 1])
    o_ref[...] = jnp.where(lane < V_DIM, outs[0], pltpu.roll(outs[1], V_DIM, 1))


def _attention(q, k, v, *, seq, batch):
    tq = min(ATTN_TILE, seq)
    pairs = MLA_HEADS // 2
    return pl.pallas_call(
        functools.partial(_attn_kernel, tq=tq),
        grid=(batch, pairs, seq // tq),
        in_specs=[
            pl.BlockSpec((None, None, tq, 2 * LANES), lambda b, j, i: (b, j, i, 0)),
            pl.BlockSpec((None, None, seq, 2 * LANES), lambda b, j, i: (b, j, 0, 0)),
            pl.BlockSpec((None, None, seq, 2 * LANES), lambda b, j, i: (b, j, 0, 0)),
        ],
        out_specs=pl.BlockSpec((None, None, tq, LANES), lambda b, j, i: (b, j, i, 0)),
        out_shape=jax.ShapeDtypeStruct((batch, pairs, seq, LANES), F32),
        scratch_shapes=[
            pltpu.VMEM((2, tq, tq // 2), F32),
            pltpu.VMEM((2, tq, tq // 2), F32),
            pltpu.VMEM((2, tq, LANES), F32),
            pltpu.VMEM((2, tq, LANES), F32),
        ],
        compiler_params=_params("arbitrary", "arbitrary", "arbitrary"),
        name="causal_attention",
    )(q, k, v)


def _s5_discretize_kernel(lr_ref, li_ref, dt_ref, lre_ref, lie_ref, dte_ref, bre_ref, bim_ref,
                          ar_ref, ai_ref, bbr_ref, bbi_ref):
    def zoh(lr, li, dt):
        mag = jnp.exp(lr * dt)
        ar, ai = mag * jnp.cos(li * dt), mag * jnp.sin(li * dt)
        nr, ni = ar - 1.0, ai
        den = lr * lr + li * li
        return ar, ai, (nr * lr + ni * li) / den, (ni * lr - nr * li) / den

    ar, ai, _, _ = zoh(lr_ref[...], li_ref[...], dt_ref[...])
    ar_ref[...] = ar
    ai_ref[...] = ai
    _, _, zr, zi = zoh(lre_ref[...], lie_ref[...], dte_ref[...])
    br, bi = bre_ref[...], bim_ref[...]
    bbr_ref[...] = zr * br - zi * bi
    bbi_ref[...] = zr * bi + zi * br


def _s5_discretize(lambda_re, lambda_im, log_step, b_re, b_im):
    g, p = lambda_re.shape
    c = b_re.shape[-1]
    dt = jnp.broadcast_to(jnp.exp(log_step)[:, None], (g, p))
    expand = lambda a: jnp.repeat(a, c, axis=1)
    flat = lambda a: a.reshape(g, p * c)
    out = pl.pallas_call(
        _s5_discretize_kernel,
        out_shape=[jax.ShapeDtypeStruct((g, p), F32)] * 2 + [jax.ShapeDtypeStruct((g, p * c), F32)] * 2,
        name="s5_discretize",
    )(lambda_re, lambda_im, dt, expand(lambda_re), expand(lambda_im), expand(dt), flat(b_re), flat(b_im))
    ar, ai, bbr, bbi = out
    return ar, ai, bbr.reshape(g, p, c), bbi.reshape(g, p, c)


def _s5_kernel(x_ref, g_ref, win_ref, bm_ref, a_ref, cm_ref, d_ref, wglu_ref, o_ref,
               state_ref, u_ref, bu_ref, st_ref, y_ref, *, layer, batch, nblock, half):
    @pl.when(pl.program_id(0) == 0)
    def _():
        state_ref[...] = jnp.zeros_like(state_ref)

    x = x_ref[...]
    rows, d = x.shape
    steps = rows // batch
    h = _rms(x, g_ref[layer:layer + 1, :]).astype(BF16)
    u = _dot_split(h, win_ref[...])
    u_ref[...] = u
    ub = u.astype(BF16)
    lanes = lambda blk: slice(blk * LANES, (blk + 1) * LANES)

    def project_in(blk):
        bu_ref[blk % 2] = _dot_split(ub[:, lanes(blk)], bm_ref[blk])

    def project_out(blk):
        y_ref[:, lanes(blk)] = _dot_split(st_ref[blk % 2].astype(BF16), cm_ref[blk])

    project_in(0)
    for blk in range(nblock):
        if blk + 1 < nblock:
            project_in(blk + 1)
        if blk >= 1:
            project_out(blk - 1)
        slot = blk % 2
        ar = a_ref[blk, 0:batch, :]
        ai = a_ref[blk, batch:2 * batch, :]
        sr = state_ref[blk, 0:batch, :]
        si = state_ref[blk, batch:2 * batch, :]
        for t in range(steps):
            r = slice(t * batch, (t + 1) * batch)
            sr, si = (ar * sr - ai * si + bu_ref[slot, r, 0:half],
                      ar * si + ai * sr + bu_ref[slot, r, half:2 * half])
            st_ref[slot, r, 0:half] = sr
            st_ref[slot, r, half:2 * half] = si
        state_ref[blk, 0:batch, :] = sr
        state_ref[blk, batch:2 * batch, :] = si
    project_out(nblock - 1)
    y = y_ref[...] + d_ref[layer:layer + 1, :] * u_ref[...]
    act = jax.nn.gelu(y).astype(BF16)
    z = _dot_split(act, wglu_ref[...])
    o_ref[...] = x + z[:, :d] * jax.nn.sigmoid(z[:, d:])


def _s5(x, w, layer, *, batch):
    rows, d = x.shape
    _, nblock, _, width = w["bm"].shape
    half = width // 2
    tr = min(ROW_TILE, rows)
    return pl.pallas_call(
        functools.partial(_s5_kernel, layer=layer, batch=batch, nblock=nblock, half=half),
        grid=(rows // tr,),
        in_specs=[
            pl.BlockSpec((tr, d), lambda i: (i, 0)),
            _resident(w["norm"].shape),
            _resident(w["win"].shape, layer),
            _resident(w["bm"].shape, layer),
            _resident(w["a"].shape, layer),
            _resident(w["cm"].shape, layer),
            _resident(w["d_skip"].shape),
            _resident(w["wglu"].shape, layer),
        ],
        out_specs=pl.BlockSpec((tr, d), lambda i: (i, 0)),
        out_shape=jax.ShapeDtypeStruct((rows, d), F32),
        scratch_shapes=[
            pltpu.VMEM((nblock, 2 * batch, half), F32),
            pltpu.VMEM((tr, d), F32),
            pltpu.VMEM((2, tr, 2 * half), F32),
            pltpu.VMEM((2, tr, 2 * half), F32),
            pltpu.VMEM((tr, d), F32),
        ],
        compiler_params=_params("arbitrary"),
        name="s5_mixer",
    )(x, w["norm"], w["win"], w["bm"], w["a"], w["cm"], w["d_skip"], w["wglu"])


def _s5_weights(ssm_norm, ssm_w_in, lambda_re, lambda_im, log_step, b_re, b_im, c_re, c_im, d_skip, w_glu,
                *, batch):
    nl, groups, nstate = lambda_re.shape
    gpb = GROUPS_PER_BLOCK
    nblock = groups // gpb
    half = gpb * nstate
    merge = lambda a: a.reshape((nl * groups,) + a.shape[2:])
    ar, ai, bbr, bbi = _s5_discretize(merge(lambda_re), merge(lambda_im), merge(log_step),
                                      merge(b_re), merge(b_im))
    eye = jnp.eye(gpb, dtype=F32)

    def in_mat(bb):
        bb = bb.reshape(nl, nblock, gpb, nstate, SSM_GROUP)
        return jnp.einsum("LGgpc,hg->LGhcgp", bb, eye).reshape(nl, nblock, LANES, half)

    def out_mat(cc):
        cc = cc.reshape(nl, nblock, gpb, SSM_GROUP, nstate)
        return jnp.einsum("LGgcp,gh->LGgphc", cc, eye).reshape(nl, nblock, half, LANES)

    bm = jnp.concatenate([in_mat(bbr), in_mat(bbi)], axis=3).astype(BF16)
    cm = jnp.concatenate([out_mat(c_re), out_mat(-c_im)], axis=2).astype(BF16)
    rep = lambda a: jnp.broadcast_to(a.reshape(nl, nblock, 1, half), (nl, nblock, batch, half))
    a_mat = jnp.concatenate([rep(ar), rep(ai)], axis=2)
    return dict(norm=ssm_norm, win=ssm_w_in.astype(BF16), bm=bm, a=a_mat, cm=cm, d_skip=d_skip,
                wglu=w_glu.astype(BF16))


def kernel(x, attn_norm, mix_w_in, cq_norm, ckv_norm, w_uq, w_ukv, q_gain, k_gain, sconv_w, mix_w_out,
           ssm_norm, ssm_w_in, lambda_re, lambda_im, log_step, b_re, b_im, c_re, c_im, d_skip, w_glu,
           ffn_norm, ffn_w_up, ffn_conv_w, ffn_w_down):
    batch, seq, d = x.shape
    depth = ffn_norm.shape[0]
    tables = _rope_tables(seq, batch)
    mix_w = _mixproj_weights(attn_norm, mix_w_in, cq_norm, ckv_norm, w_uq, w_ukv, q_gain, k_gain, sconv_w)
    w_out = mix_w_out.astype(BF16)
    s5_w = _s5_weights(ssm_norm, ssm_w_in, lambda_re, lambda_im, log_step, b_re, b_im, c_re, c_im,
                       d_skip, w_glu, batch=batch)
    ffn_w = _ffn_weights(ffn_norm, ffn_w_up, ffn_conv_w, ffn_w_down)
    xt = x
    for layer in range(depth):
        i = layer // 2
        mixer = None
        if layer % 2 == 0:
            first = layer == 0
            outs = _mixproj(xt, mix_w, i, tables, batch=batch, first=first)
            q, k, v, conv = outs[:4]
            if first:
                xt = outs[4]
            mixer = (_attention(q, k, v, seq=seq, batch=batch), conv, w_out, i)
        else:
            xt = _s5(xt, s5_w, i, batch=batch)
        xt = _ffn(xt, ffn_w, layer, batch=batch, mixer=mixer, last=layer == depth - 1)
    return xt
```
